```python
import jax, jax.numpy as jnp
from jax import lax
import numpy as np

D_MODEL = 2048
BATCH = 4
SEQ = 2048
DEPTH = 2
DEC_BATCH = 32
DEC_SEQ = 16
PAST_LEN = 4096

CHUNK = 64
D_MIX = 2 * D_MODEL
D_SSD = D_MIX // 2
SSD_HEAD_DIM = 64
SSD_HEADS = D_SSD // SSD_HEAD_DIM
SSD_GROUPS = 8
SSD_HEADS_PER_GROUP = SSD_HEADS // SSD_GROUPS
D_STATE = 128
SSD_CONV = 4
D_XBC = D_SSD + 2 * SSD_GROUPS * D_STATE
D_GMLP = D_MIX - D_SSD
GMLP_GROUPS = 8
GMLP_GROUP_DIM = D_GMLP // GMLP_GROUPS
GMLP_CHUNK = 128
D_IN = D_SSD + D_XBC + SSD_HEADS + 2 * D_GMLP
D_FF = 5632
FFN_CONV = 3
EPS = 1e-6

kernel_name = "hybrid_ssd_gmlp_convffn_stream_step"


def _rmsnorm(x, w):
    xf = x.astype(jnp.float32)
    y = xf * lax.rsqrt(jnp.mean(xf * xf, axis=-1, keepdims=True) + EPS)
    return (y * w.astype(jnp.float32)).astype(x.dtype)


def _group_rmsnorm(x, w, groups):
    shp = x.shape
    xf = x.astype(jnp.float32).reshape(shp[:-1] + (groups, shp[-1] // groups))
    y = xf * lax.rsqrt(jnp.mean(xf * xf, axis=-1, keepdims=True) + EPS)
    return y.reshape(shp) * w.astype(jnp.float32)


def _causal_dwconv(x, buf, w, b):
    k = w.shape[0]
    L = x.shape[1]
    xp = jnp.concatenate([buf.astype(x.dtype), x], axis=1)
    y = xp[:, 0:L] * w[0] + b
    for i in range(1, k):
        y = y + xp[:, i:i + L] * w[i]
    return y, xp[:, L:]


def _ssd_scan(x, dt, a, bm, cm, h0, block):
    bsz, L = x.shape[0], x.shape[1]
    nc = L // block
    G, E, P, N = SSD_GROUPS, SSD_HEADS_PER_GROUP, SSD_HEAD_DIM, D_STATE
    xc = x.reshape(bsz, nc, block, G, E, P)
    dtc = dt.reshape(bsz, nc, block, G, E)
    bc = bm.reshape(bsz, nc, block, G, N)
    cc = cm.reshape(bsz, nc, block, G, N)
    acs = jnp.cumsum(dtc * a.reshape(G, E), axis=2)
    xdt = xc * dtc[..., None]
    causal = jnp.tril(jnp.ones((block, block), dtype=bool))
    seg = acs[:, :, :, None] - acs[:, :, None, :]
    decay = jnp.exp(jnp.where(causal[:, :, None, None], seg, -jnp.inf))
    cb = jnp.einsum('bcign,bcjgn->bcijg', cc, bc)
    y_diag = jnp.einsum('bcijge,bcjgep->bcigep', cb[..., None] * decay, xdt)
    decay_end = jnp.exp(acs[:, :, -1:] - acs)
    states = jnp.einsum('bcjgn,bcjgep->bcgepn', bc, xdt * decay_end[..., None])
    chunk_decay = jnp.exp(acs[:, :, -1])

    def step(h, inp):
        s, d = inp
        return d[..., None, None] * h + s, h

    h_final, h_prev = lax.scan(step, h0.reshape(bsz, G, E, P, N),
                               (jnp.moveaxis(states, 1, 0), jnp.moveaxis(chunk_decay, 1, 0)))
    h_prev = jnp.moveaxis(h_prev, 0, 1)
    y_off = jnp.einsum('bcign,bcgepn->bcigep', cc, h_prev) * jnp.exp(acs)[..., None]
    y = (y_diag + y_off).reshape(bsz, L, SSD_HEADS, P)
    return y, h_final.reshape(bsz, SSD_HEADS, P, N)


def _spatial_gate(u, v_n, w_s, b_s):
    bsz, L, _ = v_n.shape
    blk = min(GMLP_CHUNK, L)
    nc = L // blk
    vc = v_n.reshape(bsz, nc, blk, GMLP_GROUPS, GMLP_GROUP_DIM)
    pos = jnp.arange(blk)
    mask = (pos[None, :] // CHUNK) <= (pos[:, None] // CHUNK)
    w = jnp.where(mask[None], w_s[:, :blk, :blk], 0)
    s = jnp.einsum('gij,bcjgd->bcigd', w, vc) + b_s[:, :blk].T[None, None, :, :, None]
    return u * s.reshape(bsz, L, D_GMLP)


def _trunk_layer(x, conv_buf, h0, ffn_buf, norm1_w, w_in, ssd_conv_w, ssd_conv_b, dt_bias, a_log,
                 ssd_d, ssd_norm_w, gmlp_norm_w, gmlp_w_s, gmlp_b_s, w_out, norm2_w, w_up,
                 ffn_conv_w, ffn_conv_b, w_down):
    bsz, L, _ = x.shape
    f32 = jnp.float32
    h = _rmsnorm(x, norm1_w)
    proj = jnp.einsum('bld,de->ble', h, w_in)
    s1 = D_SSD
    s2 = s1 + D_XBC
    s3 = s2 + SSD_HEADS
    s4 = s3 + D_GMLP
    z, xbc, dt_raw, u, v = proj[..., :s1], proj[..., s1:s2], proj[..., s2:s3], proj[..., s3:s4], proj[..., s4:]
    xbc, new_conv = _causal_dwconv(xbc, conv_buf, ssd_conv_w, ssd_conv_b)
    xbc = jax.nn.silu(xbc).astype(f32)
    xs = xbc[..., :D_SSD].reshape(bsz, L, SSD_HEADS, SSD_HEAD_DIM)
    bm = xbc[..., D_SSD:D_SSD + SSD_GROUPS * D_STATE].reshape(bsz, L, SSD_GROUPS, D_STATE)
    cm = xbc[..., D_SSD + SSD_GROUPS * D_STATE:].reshape(bsz, L, SSD_GROUPS, D_STATE)
    dt = jax.nn.softplus(dt_raw.astype(f32) + dt_bias.astype(f32))
    a = -jnp.exp(a_log.astype(f32))
    y, h_new = _ssd_scan(xs, dt, a, bm, cm, h0.astype(f32), min(CHUNK, L))
    y = y + ssd_d.astype(f32)[:, None] * xs
    y = y.reshape(bsz, L, D_SSD) * jax.nn.silu(z.astype(f32))
    y_ssd = _group_rmsnorm(y, ssd_norm_w, SSD_GROUPS).astype(x.dtype)
    u = jax.nn.gelu(u)
    v_n = _group_rmsnorm(jax.nn.gelu(v), gmlp_norm_w, GMLP_GROUPS).astype(x.dtype)
    y_gmlp = _spatial_gate(u, v_n, gmlp_w_s, gmlp_b_s).astype(x.dtype)
    mix = jnp.concatenate([y_ssd, y_gmlp], axis=-1)
    x = x + jnp.einsum('ble,ed->bld', mix, w_out)
    h = _rmsnorm(x, norm2_w)
    up = jnp.einsum('bld,df->blf', h, w_up)
    up, new_ffn = _causal_dwconv(up, ffn_buf, ffn_conv_w, ffn_conv_b)
    g, val = up[..., :D_FF], up[..., D_FF:]
    x = x + jnp.einsum('blf,fd->bld', jax.nn.silu(g) * val, w_down)
    return x, new_conv, h_new, new_ffn, v_n


def setup_inputs(seed: int = 0) -> dict:
    key = jax.random.key(seed)
    ks = jax.random.split(key, 24)
    f32 = jnp.float32
    nrm = lambda k, shp, s: jax.random.normal(k, shp, f32) * s
    dt0 = jnp.exp(jax.random.uniform(ks[8], (DEPTH, SSD_HEADS), f32, np.log(1e-3), np.log(1e-1)))
    return {
        'x_prompt': nrm(ks[0], (BATCH, SEQ, D_MODEL), 1.0),
        'x_sample': nrm(ks[1], (DEC_BATCH, DEC_SEQ, D_MODEL), 1.0),
        'state_ssd_conv': nrm(ks[2], (DEPTH, DEC_BATCH, SSD_CONV - 1, D_XBC), 1.0),
        'state_ssd': nrm(ks[3], (DEPTH, DEC_BATCH, SSD_HEADS, SSD_HEAD_DIM, D_STATE), 0.1),
        'state_ffn_conv': nrm(ks[4], (DEPTH, DEC_BATCH, FFN_CONV - 1, 2 * D_FF), 1.0),
        'norm1_w': 1.0 + nrm(ks[5], (DEPTH, D_MODEL), 0.02),
        'w_in': nrm(ks[6], (DEPTH, D_MODEL, D_IN), D_MODEL ** -0.5),
        'ssd_conv_w': nrm(ks[7], (DEPTH, SSD_CONV, D_XBC), 0.5),
        'ssd_conv_b': nrm(ks[9], (DEPTH, D_XBC), 0.02),
        'dt_bias': dt0 + jnp.log(-jnp.expm1(-dt0)),
        'a_log': jnp.log(jax.random.uniform(ks[10], (DEPTH, SSD_HEADS), f32, 1.0, 16.0)),
        'ssd_d': 1.0 + nrm(ks[11], (DEPTH, SSD_HEADS), 0.1),
        'ssd_norm_w': 1.0 + nrm(ks[12], (DEPTH, D_SSD), 0.02),
        'gmlp_norm_w': 1.0 + nrm(ks[13], (DEPTH, D_GMLP), 0.02),
        'gmlp_w_s': nrm(ks[14], (DEPTH, GMLP_GROUPS, GMLP_CHUNK, GMLP_CHUNK), GMLP_CHUNK ** -0.5),
        'gmlp_b_s': 1.0 + nrm(ks[15], (DEPTH, GMLP_GROUPS, GMLP_CHUNK), 0.1),
        'w_out': nrm(ks[16], (DEPTH, D_MIX, D_MODEL), D_MIX ** -0.5),
        'norm2_w': 1.0 + nrm(ks[17], (DEPTH, D_MODEL), 0.02),
        'w_up': nrm(ks[18], (DEPTH, D_MODEL, 2 * D_FF), D_MODEL ** -0.5),
        'ffn_conv_w': nrm(ks[19], (DEPTH, FFN_CONV, 2 * D_FF), 0.6),
        'ffn_conv_b': nrm(ks[20], (DEPTH, 2 * D_FF), 0.02),
        'w_down': nrm(ks[21], (DEPTH, D_FF, D_MODEL), D_FF ** -0.5),
        'final_norm_w': 1.0 + nrm(ks[22], (D_MODEL,), 0.02),
    }


def reference(x_prompt, x_sample, state_ssd_conv, state_ssd, state_ffn_conv, norm1_w, w_in, ssd_conv_w,
              ssd_conv_b, dt_bias, a_log, ssd_d, ssd_norm_w, gmlp_norm_w, gmlp_w_s, gmlp_b_s, w_out,
              norm2_w, w_up, ffn_conv_w, ffn_conv_b, w_down, final_norm_w):
    bp = x_prompt.shape[0]
    yp, ys = x_prompt, x_sample
    p_conv, p_ssd, p_ffn = [], [], []
    s_conv, s_ssd, s_ffn, s_v = [], [], [], []
    for l in range(DEPTH):
        lw = (norm1_w[l], w_in[l], ssd_conv_w[l], ssd_conv_b[l], dt_bias[l], a_log[l], ssd_d[l],
              ssd_norm_w[l], gmlp_norm_w[l], gmlp_w_s[l], gmlp_b_s[l], w_out[l], norm2_w[l], w_up[l],
              ffn_conv_w[l], ffn_conv_b[l], w_down[l])
        yp, c, hs, f, _ = _trunk_layer(
            yp,
            jnp.zeros((bp, SSD_CONV - 1, D_XBC), yp.dtype),
            jnp.zeros((bp, SSD_HEADS, SSD_HEAD_DIM, D_STATE), jnp.float32),
            jnp.zeros((bp, FFN_CONV - 1, 2 * D_FF), yp.dtype),
            *lw)
        p_conv.append(c)
        p_ssd.append(hs)
        p_ffn.append(f)
        ys, c, hs, f, vn = _trunk_layer(ys, state_ssd_conv[l], state_ssd[l], state_ffn_conv[l], *lw)
        s_conv.append(c)
        s_ssd.append(hs)
        s_ffn.append(f)
        s_v.append(vn)
    yp = _rmsnorm(yp, final_norm_w)
    ys = _rmsnorm(ys, final_norm_w)
    return (yp, ys, jnp.stack(p_conv), jnp.stack(p_ssd), jnp.stack(p_ffn),
            jnp.stack(s_conv), jnp.stack(s_ssd), jnp.stack(s_ffn), jnp.stack(s_v))
```

```python
import functools

import jax
import jax.numpy as jnp
from jax import lax
from jax.experimental import pallas as pl
from jax.experimental.pallas import tpu as pltpu

D_MODEL = 2048
DEPTH = 2
CHUNK = 64
D_SSD = 2048
SSD_HEAD_DIM = 64
SSD_HEADS = 32
SSD_GROUPS = 8
HEADS_PER_GROUP = 4
D_STATE = 128
SSD_CONV = 4
D_XBC = 4096
D_GMLP = 2048
GMLP_GROUPS = 8
GMLP_GROUP_DIM = 256
GMLP_CHUNK = 128
D_FF = 5632
FFN_CONV = 3
EPS = 1e-6

GROUP_W = 256
SSD_COLS = 896
XBC_G = 512
HALO = 8
TM = 512
SSD_Q = 256
TF = 512
NF = D_FF // TF
VMEM_LIMIT = 56 * 1024 * 1024

F32 = jnp.float32
BF16 = jnp.bfloat16


def _dot(a, b):
    return jnp.dot(a, b, preferred_element_type=F32)


def _dot_nt(a, b):
    return lax.dot_general(a, b, (((1,), (1,)), ((), ())), preferred_element_type=F32)


def _dot_tn(a, b):
    return lax.dot_general(a, b, (((0,), (0,)), ((), ())), preferred_element_type=F32)


def _rms(x, w):
    return x * lax.rsqrt(jnp.mean(x * x, axis=-1, keepdims=True) + EPS) * w


def _silu(x):
    return x * jax.nn.sigmoid(x)


def _softplus(x):
    return jnp.maximum(x, 0.0) + jnp.log1p(jnp.exp(-jnp.abs(x)))


def _seg_cumsum(x, seg_len):
    pos = lax.broadcasted_iota(jnp.int32, x.shape, 0) & (seg_len - 1)
    k = 1
    while k < seg_len:
        x = x + jnp.where(pos >= k, pltpu.roll(x, k, 0), 0.0)
        k *= 2
    return x


def _seg_total(x, seg_len):
    n = x.shape[0]
    pos = lax.broadcasted_iota(jnp.int32, x.shape, 0) & (seg_len - 1)
    k = 1
    while k < seg_len:
        x = x + jnp.where((pos & k) == 0, pltpu.roll(x, n - k, 0), pltpu.roll(x, k, 0))
        k *= 2
    return x


def _head_cols(v, width):
    n = v.shape[0]
    lane_head = lax.broadcasted_iota(jnp.int32, (n, width), 1) >> 6
    out = jnp.broadcast_to(v[:, 3:4], (n, width))
    for e in (2, 1, 0):
        out = jnp.where(lane_head == e, jnp.broadcast_to(v[:, e:e + 1], (n, width)), out)
    return out


def _gmlp_kernel(x_ref, n1_ref, w_ref, gnw_ref, ws_ref, bs_ref, wo_ref, xo_ref, h_ref, *rest,
                 chunk, seg_len, emit_vn):
    vn_ref = rest[0] if emit_vn else None
    h_scr = rest[-1]
    g = pl.program_id(1)
    tm = x_ref.shape[0]
    seg_shift = seg_len.bit_length() - 1
    chunk_shift = CHUNK.bit_length() - 1

    @pl.when(g == 0)
    def _():
        x = x_ref[...]
        hb = _rms(x, n1_ref[...]).astype(BF16)
        h_scr[...] = hb
        h_ref[...] = hb
        xo_ref[...] = x

    p = _dot(h_scr[...], w_ref[0])
    u = jax.nn.gelu(p[:, :GROUP_W])
    v = jax.nn.gelu(p[:, GROUP_W:])
    vn = _rms(v, gnw_ref[0])
    if emit_vn:
        vn_ref[...] = vn
    ii = lax.broadcasted_iota(jnp.int32, (chunk, chunk), 0)
    jj = lax.broadcasted_iota(jnp.int32, (chunk, chunk), 1)
    same_seq = (ii >> seg_shift) == (jj >> seg_shift)
    block_causal = ((jj & (seg_len - 1)) >> chunk_shift) <= ((ii & (seg_len - 1)) >> chunk_shift)
    wm = jnp.where(same_seq & block_causal, ws_ref[0], 0.0).astype(BF16)
    vnb = vn.astype(BF16)
    bias = bs_ref[0]
    parts = []
    for c in range(tm // chunk):
        s = _dot(wm, vnb[c * chunk:(c + 1) * chunk]) + bias
        parts.append(u[c * chunk:(c + 1) * chunk] * s)
    y = parts[0] if len(parts) == 1 else jnp.concatenate(parts, axis=0)
    xo_ref[...] += _dot(y.astype(BF16), wo_ref[0])


def _gmlp_call(x, n1, w_gm, gnw, ws, bs, wo_gm, *, tm, chunk, seg_len, emit_vn):
    m = x.shape[0]
    nt = m // tm
    kern = functools.partial(_gmlp_kernel, chunk=chunk, seg_len=seg_len, emit_vn=emit_vn)
    out_shape = [jax.ShapeDtypeStruct((m, D_MODEL), F32), jax.ShapeDtypeStruct((m, D_MODEL), BF16)]
    out_specs = [pl.BlockSpec((tm, D_MODEL), lambda i, g: (i, 0)), pl.BlockSpec((tm, D_MODEL), lambda i, g: (i, 0))]
    if emit_vn:
        out_shape.append(jax.ShapeDtypeStruct((m, D_GMLP), F32))
        out_specs.append(pl.BlockSpec((tm, GROUP_W), lambda i, g: (i, g)))
    return pl.pallas_call(
        kern,
        out_shape=tuple(out_shape),
        grid=(nt, GMLP_GROUPS),
        in_specs=[
            pl.BlockSpec((tm, D_MODEL), lambda i, g: (i, 0)),
            pl.BlockSpec((1, D_MODEL), lambda i, g: (0, 0)),
            pl.BlockSpec((1, D_MODEL, 2 * GROUP_W), lambda i, g: (g, 0, 0)),
            pl.BlockSpec((1, 1, GROUP_W), lambda i, g: (g, 0, 0)),
            pl.BlockSpec((1, chunk, chunk), lambda i, g: (g, 0, 0)),
            pl.BlockSpec((1, chunk, 1), lambda i, g: (g, 0, 0)),
            pl.BlockSpec((1, GROUP_W, D_MODEL), lambda i, g: (g, 0, 0)),
        ],
        out_specs=tuple(out_specs),
        scratch_shapes=[pltpu.VMEM((tm, D_MODEL), BF16)],
        compiler_params=pltpu.CompilerParams(
            dimension_semantics=("arbitrary", "arbitrary"), vmem_limit_bytes=VMEM_LIMIT),
        name="gmlp_group",
    )(x, n1, w_gm, gnw, ws, bs, wo_gm)


def _ssd_project(h_ref, w_ref, cw_ref, cb_ref, dtb_ref, alog_ref, halo, n_seq):
    tm = h_ref.shape[0]
    lt = tm // n_seq
    p = _dot(h_ref[...], w_ref[0])
    z = p[:, :GROUP_W]
    pre = p[:, GROUP_W:GROUP_W + XBC_G]
    dt = _softplus(p[:, GROUP_W + XBC_G:] + dtb_ref[0])
    cw = cw_ref[0]
    if n_seq == 1:
        xp = jnp.concatenate([halo[0], pre], axis=0)
        acc = cb_ref[0] + xp[HALO - 3:HALO - 3 + tm] * cw[0:1]
        for k in (1, 2, 3):
            acc = acc + xp[HALO - 3 + k:HALO - 3 + k + tm] * cw[k:k + 1]
        last = pre[tm - HALO:]
        last = last.reshape(1, HALO, XBC_G)
    else:
        pre3 = pre.reshape(n_seq, lt, XBC_G)
        xp = jnp.concatenate([halo, pre3], axis=1)
        acc = cb_ref[0] + xp[:, HALO - 3:HALO - 3 + lt] * cw[0:1]
        for k in (1, 2, 3):
            acc = acc + xp[:, HALO - 3 + k:HALO - 3 + k + lt] * cw[k:k + 1]
        acc = acc.reshape(tm, XBC_G)
        last = pre3[:, lt - HALO:]
    xbc = _silu(acc)
    a = -jnp.exp(alog_ref[0])
    return z, xbc, dt, dt * a, last


def _ssd_finish(y, xs, z, dvec_ref, nw_ref, wo_ref, xo_ref):
    y = y + dvec_ref[0] * xs
    y = y * _silu(z)
    y = _rms(y, nw_ref[0])
    xo_ref[...] += _dot(y.astype(BF16), wo_ref[0])


def _scan_block(acs, dt, tot, cb, xs, xsb, mask):
    q = acs.shape[0]
    acs_t = acs.T
    dt_t = dt.T
    lane_head = lax.broadcasted_iota(jnp.int32, (q, GROUP_W), 1) >> 6
    ydiag = None
    for e in range(HEADS_PER_GROUP):
        seg = acs[:, e:e + 1] - acs_t[e:e + 1, :]
        mp = cb * jnp.exp(jnp.where(mask, seg, -jnp.inf)) * dt_t[e:e + 1, :]
        xe = jnp.where(lane_head == e, xsb, jnp.zeros_like(xsb))
        d = _dot(mp.astype(BF16), xe)
        ydiag = d if ydiag is None else ydiag + d
    scale = _head_cols(jnp.exp(acs), GROUP_W)
    dend = _head_cols(jnp.exp(tot - acs) * dt, GROUP_W)
    xdtd = (xs * dend).astype(BF16)
    return ydiag, scale, xdtd


def _ssd_prompt_kernel(h_ref, xg_ref, w_ref, cw_ref, cb_ref, dtb_ref, alog_ref, dvec_ref, nw_ref, wo_ref,
                       xo_ref, cst_ref, hfin_ref, carry, state):
    t = pl.program_id(1)
    g = pl.program_id(2)
    tm = h_ref.shape[0]
    q = SSD_Q

    @pl.when(g == 0)
    def _():
        xo_ref[...] = xg_ref[...]

    @pl.when(t == 0)
    def _():
        carry[g] = jnp.zeros((HALO, XBC_G), F32)
        state[g] = jnp.zeros((GROUP_W, D_STATE), F32)

    z, xbc, dt, dta, last = _ssd_project(h_ref, w_ref, cw_ref, cb_ref, dtb_ref, alog_ref,
                                         carry[g].reshape(1, HALO, XBC_G), 1)
    carry[g] = last[0]
    cst_ref[0, 0] = last[0]
    xs = xbc[:, :GROUP_W]
    xsb = xs.astype(BF16)
    bb = xbc[:, GROUP_W:GROUP_W + D_STATE].astype(BF16)
    cc = xbc[:, GROUP_W + D_STATE:].astype(BF16)
    ii = lax.broadcasted_iota(jnp.int32, (q, q), 0)
    jj = lax.broadcasted_iota(jnp.int32, (q, q), 1)
    mask = ii >= jj
    row_head = lax.broadcasted_iota(jnp.int32, (GROUP_W, D_STATE), 0) >> 6
    hst = state[g]
    ys = []
    for c in range(tm // q):
        sl = slice(c * q, (c + 1) * q)
        acs = _seg_cumsum(dta[sl], q)
        tot = acs[q - 1:q, :]
        cb = _dot_nt(cc[sl], bb[sl])
        ydiag, scale, xdtd = _scan_block(acs, dt[sl], tot, cb, xs[sl], xsb[sl], mask)
        yoff = _dot_nt(cc[sl], hst.astype(BF16)) * scale
        ys.append(ydiag + yoff)
        s_new = _dot_tn(xdtd, bb[sl])
        cd = jnp.exp(tot)
        cdm = jnp.broadcast_to(cd[:, 3:4], (GROUP_W, D_STATE))
        for e in (2, 1, 0):
            cdm = jnp.where(row_head == e, jnp.broadcast_to(cd[:, e:e + 1], (GROUP_W, D_STATE)), cdm)
        hst = hst * cdm + s_new
    state[g] = hst
    hfin_ref[0] = hst.reshape(HEADS_PER_GROUP, SSD_HEAD_DIM, D_STATE)
    y = ys[0] if len(ys) == 1 else jnp.concatenate(ys, axis=0)
    _ssd_finish(y, xs, z, dvec_ref, nw_ref, wo_ref, xo_ref)


def _ssd_sample_kernel(h_ref, xg_ref, w_ref, cw_ref, cb_ref, dtb_ref, alog_ref, dvec_ref, nw_ref, wo_ref,
                       halo_ref, h0_ref, xo_ref, cst_ref, hfin_ref, *, n_seq):
    g = pl.program_id(1)
    tm = h_ref.shape[0]
    lt = tm // n_seq

    @pl.when(g == 0)
    def _():
        xo_ref[...] = xg_ref[...]

    z, xbc, dt, dta, last = _ssd_project(h_ref, w_ref, cw_ref, cb_ref, dtb_ref, alog_ref,
                                         halo_ref[0], n_seq)
    cst_ref[0] = last
    xs = xbc[:, :GROUP_W]
    xsb = xs.astype(BF16)
    bb = xbc[:, GROUP_W:GROUP_W + D_STATE].astype(BF16)
    cc = xbc[:, GROUP_W + D_STATE:].astype(BF16)
    ii = lax.broadcasted_iota(jnp.int32, (tm, tm), 0)
    jj = lax.broadcasted_iota(jnp.int32, (tm, tm), 1)
    lt_shift = lt.bit_length() - 1
    mask = ((ii >> lt_shift) == (jj >> lt_shift)) & (ii >= jj)
    acs = _seg_cumsum(dta, lt)
    tot = _seg_total(dta, lt)
    cb = _dot_nt(cc, bb)
    ydiag, scale, xdtd = _scan_block(acs, dt, tot, cb, xs, xsb, mask)
    row_seq = lax.broadcasted_iota(jnp.int32, (tm, D_STATE), 0) >> lt_shift
    zero = jnp.zeros((tm, D_STATE), BF16)
    h0 = [h0_ref[s].reshape(GROUP_W, D_STATE) for s in range(n_seq)]
    h0_wide = jnp.concatenate(h0, axis=1).astype(BF16)
    c_wide = jnp.concatenate([jnp.where(row_seq == s, cc, zero) for s in range(n_seq)], axis=1)
    b_wide = jnp.concatenate([jnp.where(row_seq == s, bb, zero) for s in range(n_seq)], axis=1)
    yoff = _dot_nt(c_wide, h0_wide) * scale
    s_new = _dot_tn(xdtd, b_wide)
    cd_t = jnp.exp(tot).T
    row_head = lax.broadcasted_iota(jnp.int32, (GROUP_W, D_STATE), 0) >> 6
    for s in range(n_seq):
        col = cd_t[:, s * lt:s * lt + 1]
        cdm = jnp.broadcast_to(col[3:4, :], (GROUP_W, D_STATE))
        for e in (2, 1, 0):
            cdm = jnp.where(row_head == e, jnp.broadcast_to(col[e:e + 1, :], (GROUP_W, D_STATE)), cdm)
        hn = h0[s] * cdm + s_new[:, s * D_STATE:(s + 1) * D_STATE]
        hfin_ref[s] = hn.reshape(HEADS_PER_GROUP, SSD_HEAD_DIM, D_STATE)
    _ssd_finish(ydiag + yoff, xs, z, dvec_ref, nw_ref, wo_ref, xo_ref)


def _ssd_group_specs(idx):
    return [
        pl.BlockSpec((1, D_MODEL, SSD_COLS), lambda *a: (idx(*a), 0, 0)),
        pl.BlockSpec((1, SSD_CONV, XBC_G), lambda *a: (idx(*a), 0, 0)),
        pl.BlockSpec((1, 1, XBC_G), lambda *a: (idx(*a), 0, 0)),
        pl.BlockSpec((1, 1, 128), lambda *a: (idx(*a), 0, 0)),
        pl.BlockSpec((1, 1, 128), lambda *a: (idx(*a), 0, 0)),
        pl.BlockSpec((1, 1, GROUP_W), lambda *a: (idx(*a), 0, 0)),
        pl.BlockSpec((1, 1, GROUP_W), lambda *a: (idx(*a), 0, 0)),
        pl.BlockSpec((1, GROUP_W, D_MODEL), lambda *a: (idx(*a), 0, 0)),
    ]


def _ssd_prompt_call(h, xg, gw, *, batch, tm):
    m = h.shape[0]
    nt = m // (batch * tm)
    row = lambda b, t, g: (b * nt + t, 0)
    return pl.pallas_call(
        _ssd_prompt_kernel,
        out_shape=(jax.ShapeDtypeStruct((m, D_MODEL), F32),
                   jax.ShapeDtypeStruct((batch * nt, SSD_GROUPS, HALO, XBC_G), F32),
                   jax.ShapeDtypeStruct((batch * nt, SSD_HEADS, SSD_HEAD_DIM, D_STATE), F32)),
        grid=(batch, nt, SSD_GROUPS),
        in_specs=[pl.BlockSpec((tm, D_MODEL), row), pl.BlockSpec((tm, D_MODEL), row)]
        + _ssd_group_specs(lambda b, t, g: g),
        out_specs=(
            pl.BlockSpec((tm, D_MODEL), row),
            pl.BlockSpec((1, 1, HALO, XBC_G), lambda b, t, g: (b * nt + t, g, 0, 0)),
            pl.BlockSpec((1, HEADS_PER_GROUP, SSD_HEAD_DIM, D_STATE), lambda b, t, g: (b * nt + t, g, 0, 0)),
        ),
        scratch_shapes=[pltpu.VMEM((SSD_GROUPS, HALO, XBC_G), F32),
                        pltpu.VMEM((SSD_GROUPS, GROUP_W, D_STATE), F32)],
        compiler_params=pltpu.CompilerParams(
            dimension_semantics=("arbitrary", "arbitrary", "arbitrary"), vmem_limit_bytes=VMEM_LIMIT),
        name="ssd_prompt_group",
    )(h, xg, *gw)


def _ssd_sample_call(h, xg, gw, halo, h0, *, n_seq, lt):
    m = h.shape[0]
    tm = n_seq * lt
    nt = m // tm
    n_all = m // lt
    kern = functools.partial(_ssd_sample_kernel, n_seq=n_seq)
    row = lambda i, g: (i, 0)
    return pl.pallas_call(
        kern,
        out_shape=(jax.ShapeDtypeStruct((m, D_MODEL), F32),
                   jax.ShapeDtypeStruct((SSD_GROUPS, n_all, HALO, XBC_G), F32),
                   jax.ShapeDtypeStruct((n_all, SSD_HEADS, SSD_HEAD_DIM, D_STATE), F32)),
        grid=(nt, SSD_GROUPS),
        in_specs=[pl.BlockSpec((tm, D_MODEL), row), pl.BlockSpec((tm, D_MODEL), row)]
        + _ssd_group_specs(lambda i, g: g)
        + [pl.BlockSpec((1, n_seq, HALO, XBC_G), lambda i, g: (g, i, 0, 0)),
           pl.BlockSpec((n_seq, HEADS_PER_GROUP, SSD_HEAD_DIM, D_STATE), lambda i, g: (i, g, 0, 0))],
        out_specs=(
            pl.BlockSpec((tm, D_MODEL), row),
            pl.BlockSpec((1, n_seq, HALO, XBC_G), lambda i, g: (g, i, 0, 0)),
            pl.BlockSpec((n_seq, HEADS_PER_GROUP, SSD_HEAD_DIM, D_STATE), lambda i, g: (i, g, 0, 0)),
        ),
        compiler_params=pltpu.CompilerParams(
            dimension_semantics=("arbitrary", "arbitrary"), vmem_limit_bytes=VMEM_LIMIT),
        name="ssd_sample_group",
    )(h, xg, *gw, halo, h0)


def _ffn_kernel(*refs, n_seq, carried, final):
    if carried:
        (x_ref, n2_ref, wg_ref, wv_ref, cwg_ref, cwv_ref, cbg_ref, cbv_ref, wd_ref, fn_ref,
         xo_ref, lg_ref, lv_ref, h_scr, carry_g, carry_v) = refs
    else:
        (x_ref, n2_ref, wg_ref, wv_ref, cwg_ref, cwv_ref, cbg_ref, cbv_ref, wd_ref, fn_ref, hg_ref, hv_ref,
         xo_ref, lg_ref, lv_ref, h_scr) = refs
    nax = 3 if carried else 2
    f = pl.program_id(nax - 1)
    tm = x_ref.shape[0]
    lt = tm // n_seq
    tf = wg_ref.shape[1]

    @pl.when(f == 0)
    def _():
        x = x_ref[...]
        h_scr[...] = _rms(x, n2_ref[...]).astype(BF16)
        xo_ref[...] = x

    if carried:
        @pl.when(pl.program_id(1) == 0)
        def _():
            carry_g[f] = jnp.zeros((HALO, tf), F32)
            carry_v[f] = jnp.zeros((HALO, tf), F32)
        halo_g = carry_g[f].reshape(1, HALO, tf)
        halo_v = carry_v[f].reshape(1, HALO, tf)
    else:
        halo_g = hg_ref[...]
        halo_v = hv_ref[...]

    hb = h_scr[...]

    def conv(u, halo, cw, cb):
        if n_seq == 1:
            xp = jnp.concatenate([halo[0], u], axis=0)
            acc = cb + xp[HALO - 2:HALO - 2 + tm] * cw[0:1]
            acc = acc + xp[HALO - 1:HALO - 1 + tm] * cw[1:2]
            acc = acc + xp[HALO:] * cw[2:3]
            return acc, u[tm - HALO:].reshape(1, HALO, tf)
        u3 = u.reshape(n_seq, lt, tf)
        xp = jnp.concatenate([halo, u3], axis=1)
        acc = cb + xp[:, HALO - 2:HALO - 2 + lt] * cw[0:1]
        acc = acc + xp[:, HALO - 1:HALO - 1 + lt] * cw[1:2]
        acc = acc + xp[:, HALO:] * cw[2:3]
        return acc.reshape(tm, tf), u3[:, lt - HALO:]

    cg, last_g = conv(_dot(hb, wg_ref[...]), halo_g, cwg_ref[...], cbg_ref[...])
    cv, last_v = conv(_dot(hb, wv_ref[...]), halo_v, cwv_ref[...], cbv_ref[...])
    lg_ref[...] = last_g
    lv_ref[...] = last_v
    if carried:
        carry_g[f] = last_g[0]
        carry_v[f] = last_v[0]
    act = (_silu(cg) * cv).astype(BF16)
    xo_ref[...] += _dot(act, wd_ref[...])

    if final:
        @pl.when(f == pl.num_programs(nax - 1) - 1)
        def _():
            xo_ref[...] = _rms(xo_ref[...], fn_ref[...])


def _ffn_call(x, n2, w_up, cw, cb, w_down, fnw, halos, *, batch, tm, n_seq, final):
    m = x.shape[0]
    carried = halos is None
    kern = functools.partial(_ffn_kernel, n_seq=n_seq, carried=carried, final=final)
    if carried:
        nt = m // (batch * tm)
        grid = (batch, nt, NF)
        row = lambda b, t, f: (b * nt + t, 0)
        fi = lambda b, t, f: f
        si = lambda b, t, f: b * nt + t
        n_state = batch * nt
    else:
        grid = (m // tm, NF)
        row = lambda i, f: (i, 0)
        fi = lambda i, f: f
        si = lambda i, f: i
        n_state = m // (tm // n_seq)
    col = lambda off: (lambda *a: (0, fi(*a) + off))
    in_specs = [
        pl.BlockSpec((tm, D_MODEL), row),
        pl.BlockSpec((1, D_MODEL), lambda *a: (0, 0)),
        pl.BlockSpec((D_MODEL, TF), col(0)),
        pl.BlockSpec((D_MODEL, TF), col(NF)),
        pl.BlockSpec((FFN_CONV, TF), col(0)),
        pl.BlockSpec((FFN_CONV, TF), col(NF)),
        pl.BlockSpec((1, TF), col(0)),
        pl.BlockSpec((1, TF), col(NF)),
        pl.BlockSpec((TF, D_MODEL), lambda *a: (fi(*a), 0)),
        pl.BlockSpec((1, D_MODEL), lambda *a: (0, 0)),
    ]
    args = [x, n2, w_up, w_up, cw, cw, cb, cb, w_down, fnw]
    scratch = [pltpu.VMEM((tm, D_MODEL), BF16)]
    if carried:
        scratch += [pltpu.VMEM((NF, HALO, TF), F32), pltpu.VMEM((NF, HALO, TF), F32)]
    else:
        in_specs += [pl.BlockSpec((n_seq, HALO, TF), lambda *a: (si(*a), 0, fi(*a))),
                     pl.BlockSpec((n_seq, HALO, TF), lambda *a: (si(*a), 0, fi(*a) + NF))]
        args += [halos, halos]
    st_rows = 1 if carried else n_seq
    return pl.pallas_call(
        kern,
        out_shape=(jax.ShapeDtypeStruct((m, D_MODEL), F32),
                   jax.ShapeDtypeStruct((n_state, HALO, D_FF), F32),
                   jax.ShapeDtypeStruct((n_state, HALO, D_FF), F32)),
        grid=grid,
        in_specs=in_specs,
        out_specs=(
            pl.BlockSpec((tm, D_MODEL), row),
            pl.BlockSpec((st_rows, HALO, TF), lambda *a: (si(*a), 0, fi(*a))),
            pl.BlockSpec((st_rows, HALO, TF), lambda *a: (si(*a), 0, fi(*a))),
        ),
        scratch_shapes=scratch,
        compiler_params=pltpu.CompilerParams(
            dimension_semantics=("arbitrary",) * len(grid), vmem_limit_bytes=VMEM_LIMIT),
        name="ffn_prompt" if carried else "ffn_sample",
    )(*args)


def _group_cols(w, n_groups):
    return w.reshape(w.shape[:-1] + (n_groups, w.shape[-1] // n_groups))


def _xbc_to_groups(a):
    xs = _group_cols(a[..., :D_SSD], SSD_GROUPS)
    bs = _group_cols(a[..., D_SSD:D_SSD + SSD_GROUPS * D_STATE], SSD_GROUPS)
    cs = _group_cols(a[..., D_SSD + SSD_GROUPS * D_STATE:], SSD_GROUPS)
    return jnp.concatenate([xs, bs, cs], axis=-1)


def _groups_to_xbc(a):
    lead = a.shape[:-2]
    xs = a[..., :GROUP_W].reshape(lead + (D_SSD,))
    bs = a[..., GROUP_W:GROUP_W + D_STATE].reshape(lead + (SSD_GROUPS * D_STATE,))
    cs = a[..., GROUP_W + D_STATE:].reshape(lead + (SSD_GROUPS * D_STATE,))
    return jnp.concatenate([xs, bs, cs], axis=-1)


def _layer_weights(l, norm1_w, w_in, ssd_conv_w, ssd_conv_b, dt_bias, a_log, ssd_d, ssd_norm_w, gmlp_norm_w,
                   gmlp_w_s, gmlp_b_s, w_out, norm2_w, w_up, ffn_conv_w, ffn_conv_b, w_down):
    wi = w_in[l]
    s1, s2, s3, s4 = D_SSD, D_SSD + D_XBC, D_SSD + D_XBC + SSD_HEADS, D_SSD + D_XBC + SSD_HEADS + D_GMLP
    wz = _group_cols(wi[:, :s1], SSD_GROUPS)
    wxbc = _xbc_to_groups(wi[:, s1:s2])
    wdt = _group_cols(wi[:, s2:s3], SSD_GROUPS)
    wdt = jnp.pad(wdt, ((0, 0), (0, 0), (0, 128 - HEADS_PER_GROUP)))
    w_ssd = jnp.concatenate([wz, wxbc, wdt], axis=-1).transpose(1, 0, 2).astype(BF16)
    wu = _group_cols(wi[:, s3:s4], GMLP_GROUPS)
    wv = _group_cols(wi[:, s4:], GMLP_GROUPS)
    w_gm = jnp.concatenate([wu, wv], axis=-1).transpose(1, 0, 2).astype(BF16)
    wo = w_out[l].astype(BF16).reshape(2 * SSD_GROUPS, GROUP_W, D_MODEL)
    pad_h = ((0, 0), (0, 128 - HEADS_PER_GROUP))
    ssd_w = (
        w_ssd,
        _xbc_to_groups(ssd_conv_w[l]).transpose(1, 0, 2),
        _xbc_to_groups(ssd_conv_b[l])[:, None, :],
        jnp.pad(dt_bias[l].reshape(SSD_GROUPS, HEADS_PER_GROUP), pad_h)[:, None, :],
        jnp.pad(a_log[l].reshape(SSD_GROUPS, HEADS_PER_GROUP), pad_h)[:, None, :],
        jnp.repeat(ssd_d[l], SSD_HEAD_DIM).reshape(SSD_GROUPS, 1, GROUP_W),
        ssd_norm_w[l].reshape(SSD_GROUPS, 1, GROUP_W),
        wo[:SSD_GROUPS],
    )
    gm_w = dict(
        n1=norm1_w[l][None, :], w=w_gm, gnw=gmlp_norm_w[l].reshape(GMLP_GROUPS, 1, GROUP_W),
        ws=gmlp_w_s[l], bs=gmlp_b_s[l], wo=wo[SSD_GROUPS:],
    )
    ffn_w = dict(
        n2=norm2_w[l][None, :], w_up=w_up[l].astype(BF16), cw=ffn_conv_w[l], cb=ffn_conv_b[l][None, :],
        w_down=w_down[l].astype(BF16),
    )
    return gm_w, ssd_w, ffn_w


def _pad_halo(state):
    return jnp.pad(state, ((0, 0), (HALO - state.shape[1], 0), (0, 0)))


def kernel(x_prompt, x_sample, state_ssd_conv, state_ssd, state_ffn_conv, norm1_w, w_in, ssd_conv_w, ssd_conv_b,
           dt_bias, a_log, ssd_d, ssd_norm_w, gmlp_norm_w, gmlp_w_s, gmlp_b_s, w_out, norm2_w, w_up, ffn_conv_w,
           ffn_conv_b, w_down, final_norm_w):
    bp, lp, _ = x_prompt.shape
    bs, ls, _ = x_sample.shape
    xp = x_prompt.reshape(bp * lp, D_MODEL)
    xs = x_sample.reshape(bs * ls, D_MODEL)
    fnw = final_norm_w[None, :]
    ssd_seqs = 16
    p_conv, p_ssd, p_ffn = [], [], []
    s_conv, s_ssd, s_ffn, s_v = [], [], [], []
    for l in range(DEPTH):
        gm_w, ssd_w, ffn_w = _layer_weights(
            l, norm1_w, w_in, ssd_conv_w, ssd_conv_b, dt_bias, a_log, ssd_d, ssd_norm_w, gmlp_norm_w, gmlp_w_s,
            gmlp_b_s, w_out, norm2_w, w_up, ffn_conv_w, ffn_conv_b, w_down)
        final = l == DEPTH - 1
        xg, h = _gmlp_call(xp, gm_w["n1"], gm_w["w"], gm_w["gnw"], gm_w["ws"], gm_w["bs"][:, :, None],
                           gm_w["wo"], tm=TM, chunk=GMLP_CHUNK, seg_len=GMLP_CHUNK, emit_vn=False)
        x1, cst, hfin = _ssd_prompt_call(h, xg, ssd_w, batch=bp, tm=TM)
        xp, lg, lv = _ffn_call(x1, ffn_w["n2"], ffn_w["w_up"], ffn_w["cw"], ffn_w["cb"], ffn_w["w_down"], fnw,
                               None, batch=bp, tm=TM, n_seq=1, final=final)
        ntp = lp // TM
        cst, hfin, lg, lv = (a.reshape((bp, ntp) + a.shape[1:])[:, -1] for a in (cst, hfin, lg, lv))
        p_conv.append(_groups_to_xbc(cst[:, :, HALO - (SSD_CONV - 1):, :].transpose(0, 2, 1, 3)))
        p_ssd.append(hfin)
        p_ffn.append(jnp.concatenate([lg, lv], axis=-1)[:, HALO - (FFN_CONV - 1):, :])
        ms = bs * ls
        ws_t = jnp.tile(gmlp_w_s[l][:, :ls, :ls], (1, bs, bs))
        bs_t = jnp.tile(gmlp_b_s[l][:, :ls], (1, bs))[:, :, None]
        xg, h, vn = _gmlp_call(xs, gm_w["n1"], gm_w["w"], gm_w["gnw"], ws_t, bs_t, gm_w["wo"],
                               tm=ms, chunk=ms, seg_len=ls, emit_vn=True)
        halo = _pad_halo(_xbc_to_groups(state_ssd_conv[l]).transpose(2, 0, 1, 3).reshape(
            SSD_GROUPS * bs, SSD_CONV - 1, XBC_G)).reshape(SSD_GROUPS, bs, HALO, XBC_G)
        x1, cst, hfin = _ssd_sample_call(h, xg, ssd_w, halo, state_ssd[l], n_seq=ssd_seqs, lt=ls)
        fh = _pad_halo(state_ffn_conv[l])
        xs, lg, lv = _ffn_call(x1, ffn_w["n2"], ffn_w["w_up"], ffn_w["cw"], ffn_w["cb"], ffn_w["w_down"], fnw,
                               fh, batch=1, tm=ms, n_seq=bs, final=final)
        s_conv.append(_groups_to_xbc(cst[:, :, HALO - (SSD_CONV - 1):, :].transpose(1, 2, 0, 3)))
        s_ssd.append(hfin)
        s_ffn.append(jnp.concatenate([lg, lv], axis=-1)[:, HALO - (FFN_CONV - 1):, :])
        s_v.append(vn.reshape(bs, ls, D_GMLP))
    return (xp.reshape(bp, lp, D_MODEL), xs.reshape(bs, ls, D_MODEL),
            jnp.stack(p_conv), jnp.stack(p_ssd), jnp.stack(p_ffn),
            jnp.stack(s_conv), jnp.stack(s_ssd), jnp.stack(s_ffn), jnp.stack(s_v))
```

```python
import functools

import jax
import jax.numpy as jnp
from jax import lax
from jax.experimental import pallas as pl
from jax.experimental.pallas import tpu as pltpu

D_MODEL = 2048
DEPTH = 2
CHUNK = 64
D_SSD = 2048
SSD_HEAD_DIM = 64
SSD_HEADS = 32
SSD_GROUPS = 8
HEADS_PER_GROUP = 4
D_STATE = 128
SSD_CONV = 4
D_XBC = 4096
D_GMLP = 2048
GMLP_GROUPS = 8
GMLP_CHUNK = 128
D_FF = 5632
FFN_CONV = 3
EPS = 1e-6

LANES = 128
GROUP_W = 256
XBC_G = GROUP_W + 2 * D_STATE
HALO = 8
TM = 512
SSD_Q = 256
TF = 512
NF = D_FF // TF
VMEM_LIMIT = 56 * 1024 * 1024

S_XBC = D_SSD
S_B = S_XBC + D_SSD
S_C = S_B + SSD_GROUPS * D_STATE
S_DT = S_C + SSD_GROUPS * D_STATE
S_U = S_DT + SSD_HEADS
W_SSD_COLS = S_DT + LANES

F32 = jnp.float32
BF16 = jnp.bfloat16


def _dot(a, b):
    return jnp.dot(a, b, preferred_element_type=F32)


def _dot_nt(a, b):
    return lax.dot_general(a, b, (((1,), (1,)), ((), ())), preferred_element_type=F32)


def _dot_tn(a, b):
    return lax.dot_general(a, b, (((0,), (0,)), ((), ())), preferred_element_type=F32)


def _rms(x, w):
    return x * lax.rsqrt(jnp.mean(x * x, axis=-1, keepdims=True) + EPS) * w


def _silu(x):
    return x * jax.nn.sigmoid(x)


def _softplus(x):
    return jnp.maximum(x, 0.0) + jnp.log1p(jnp.exp(-jnp.abs(x)))


def _seg_cumsum(x, seg_len):
    pos = lax.broadcasted_iota(jnp.int32, x.shape, 0) & (seg_len - 1)
    k = 1
    while k < seg_len:
        x = x + jnp.where(pos >= k, pltpu.roll(x, k, 0), 0.0)
        k *= 2
    return x


def _seg_total(x, seg_len):
    n = x.shape[0]
    pos = lax.broadcasted_iota(jnp.int32, x.shape, 0) & (seg_len - 1)
    k = 1
    while k < seg_len:
        x = x + jnp.where((pos & k) == 0, pltpu.roll(x, n - k, 0), pltpu.roll(x, k, 0))
        k *= 2
    return x


def _head_cols(v, width):
    n = v.shape[0]
    lane_head = lax.broadcasted_iota(jnp.int32, (n, width), 1) >> 6
    out = jnp.broadcast_to(v[:, 3:4], (n, width))
    for e in (2, 1, 0):
        out = jnp.where(lane_head == e, jnp.broadcast_to(v[:, e:e + 1], (n, width)), out)
    return out


def _head_rows(row):
    row_head = lax.broadcasted_iota(jnp.int32, (GROUP_W, D_STATE), 0) >> 6
    out = jnp.broadcast_to(row[:, 3:4], (GROUP_W, D_STATE))
    for e in (2, 1, 0):
        out = jnp.where(row_head == e, jnp.broadcast_to(row[:, e:e + 1], (GROUP_W, D_STATE)), out)
    return out


def _gmlp_kernel(x_ref, n1_ref, wu_ref, wv_ref, gnw_ref, ws_ref, bs_ref, wo_ref, xo_ref, h_ref, *rest,
                 chunk, seg_len, emit_vn):
    vn_ref = rest[0] if emit_vn else None
    h_scr, u_scr, v_scr = rest[-3:]
    s = pl.program_id(1)
    tm = x_ref.shape[0]
    seg_shift = seg_len.bit_length() - 1
    chunk_shift = CHUNK.bit_length() - 1

    def proj_u(slot):
        u_scr[slot] = _dot(h_scr[...], wu_ref[...])

    def proj_v(slot):
        v_scr[slot] = _dot(h_scr[...], wv_ref[...])

    def gate(slot, fills):
        fills = list(fills)
        if fills:
            fills.pop(0)()
        u = jax.nn.gelu(u_scr[slot])
        v = jax.nn.gelu(v_scr[slot])
        vn = _rms(v, gnw_ref[...])
        if emit_vn:
            vn_ref[...] = vn
        for fill in fills:
            fill()
        ii = lax.broadcasted_iota(jnp.int32, (chunk, chunk), 0)
        jj = lax.broadcasted_iota(jnp.int32, (chunk, chunk), 1)
        same_seq = (ii >> seg_shift) == (jj >> seg_shift)
        block_causal = ((jj & (seg_len - 1)) >> chunk_shift) <= ((ii & (seg_len - 1)) >> chunk_shift)
        w_rows = ws_ref[0]
        w_full = w_rows if seg_len == chunk else jnp.concatenate([w_rows] * (chunk // seg_len), axis=0)
        wm = jnp.where(same_seq & block_causal, w_full, 0.0).astype(BF16)
        vnb = vn.astype(BF16)
        bias = bs_ref[0]
        parts = []
        for c in range(tm // chunk):
            sg = _dot(wm, vnb[c * chunk:(c + 1) * chunk]) + bias
            parts.append(u[c * chunk:(c + 1) * chunk] * sg)
        y = parts[0] if len(parts) == 1 else jnp.concatenate(parts, axis=0)
        xo_ref[...] += _dot(y.astype(BF16), wo_ref[0])

    @pl.when(s == 0)
    def _():
        x = x_ref[...]
        hb = _rms(x, n1_ref[...]).astype(BF16)
        h_scr[...] = hb
        h_ref[...] = hb
        xo_ref[...] = x
        proj_u(0)
        proj_v(0)

    @pl.when((s > 0) & (s < GMLP_GROUPS))
    def _():
        slot = s & 1
        gate((s - 1) & 1, [functools.partial(proj_u, slot), functools.partial(proj_v, slot)])

    @pl.when(s == GMLP_GROUPS)
    def _():
        gate((GMLP_GROUPS - 1) & 1, [])


def _gmlp_call(x, n1, w_uv, gnw, ws, bs, wo, *, tm, chunk, seg_len, emit_vn):
    m = x.shape[0]
    nt = m // tm
    kern = functools.partial(_gmlp_kernel, chunk=chunk, seg_len=seg_len, emit_vn=emit_vn)
    out_shape = [jax.ShapeDtypeStruct((m, D_MODEL), F32), jax.ShapeDtypeStruct((m, D_MODEL), BF16)]
    out_specs = [pl.BlockSpec((tm, D_MODEL), lambda i, g: (i, 0)), pl.BlockSpec((tm, D_MODEL), lambda i, g: (i, 0))]
    g1 = lambda s: jnp.minimum(s, GMLP_GROUPS - 1)
    g2 = lambda s: jnp.maximum(s - 1, 0)
    if emit_vn:
        out_shape.append(jax.ShapeDtypeStruct((m, D_GMLP), F32))
        out_specs.append(pl.BlockSpec((tm, GROUP_W), lambda i, s: (i, g2(s))))
    return pl.pallas_call(
        kern,
        out_shape=tuple(out_shape),
        grid=(nt, GMLP_GROUPS + 1),
        in_specs=[
            pl.BlockSpec((tm, D_MODEL), lambda i, s: (i, 0)),
            pl.BlockSpec((1, D_MODEL), lambda i, s: (0, 0)),
            pl.BlockSpec((D_MODEL, GROUP_W), lambda i, s: (0, g1(s))),
            pl.BlockSpec((D_MODEL, GROUP_W), lambda i, s: (0, GMLP_GROUPS + g1(s))),
            pl.BlockSpec((1, GROUP_W), lambda i, s: (0, g2(s))),
            pl.BlockSpec((1, seg_len, chunk), lambda i, s: (g2(s), 0, 0)),
            pl.BlockSpec((1, chunk, 1), lambda i, s: (g2(s), 0, 0)),
            pl.BlockSpec((1, GROUP_W, D_MODEL), lambda i, s: (SSD_GROUPS + g2(s), 0, 0)),
        ],
        out_specs=tuple(out_specs),
        scratch_shapes=[pltpu.VMEM((tm, D_MODEL), BF16), pltpu.VMEM((2, tm, GROUP_W), F32),
                        pltpu.VMEM((2, tm, GROUP_W), F32)],
        compiler_params=pltpu.CompilerParams(
            dimension_semantics=("arbitrary", "arbitrary"), vmem_limit_bytes=VMEM_LIMIT),
        name="gmlp_group",
    )(x, n1, w_uv, w_uv, gnw, ws, bs, wo)


def _ssd_heads(h_ref, wdt_ref, dtb_ref, alog_ref, seg_len, hd):
    acs_scr, eacs_scr, dend_scr, etot_scr, acs_t_scr, dt_t_scr = hd
    dt = _softplus(_dot(h_ref[...], wdt_ref[...]) + dtb_ref[...])
    dta = dt * (-jnp.exp(alog_ref[...]))
    acs = _seg_cumsum(dta, seg_len)
    tot = _seg_total(dta, seg_len)
    acs_scr[...] = acs
    eacs_scr[...] = jnp.exp(acs)
    dend_scr[...] = jnp.exp(tot - acs) * dt
    etot_scr[...] = jnp.exp(tot)
    acs_t_scr[...] = acs.T
    dt_t_scr[...] = dt.T


def _group_lanes(v, g):
    return pltpu.roll(v, (LANES - HEADS_PER_GROUP * g) & (LANES - 1), 1)


def _ssd_stage1(h_ref, wz_ref, wx_ref, wb_ref, wc_ref, pz_scr, pxbc_scr, slot):
    tm = h_ref.shape[0]
    rows = pl.ds(HALO, tm)

    def proj_z():
        pz_scr[slot] = _dot(h_ref[...], wz_ref[...])

    def proj_x():
        pxbc_scr[slot, rows, 0:GROUP_W] = _dot(h_ref[...], wx_ref[...])

    def proj_bc():
        w_bc = jnp.concatenate([wb_ref[...], wc_ref[...]], axis=1)
        pxbc_scr[slot, rows, GROUP_W:XBC_G] = _dot(h_ref[...], w_bc)

    return [proj_z, proj_x, proj_bc]


def _conv_weights(cwx_ref, cwb_ref, cwc_ref, cbx_ref, cbb_ref, cbc_ref):
    cw = jnp.concatenate([cwx_ref[...], cwb_ref[...], cwc_ref[...]], axis=1)
    cb = jnp.concatenate([cbx_ref[...], cbb_ref[...], cbc_ref[...]], axis=1)
    return cw, cb


def _ssd_finish(y, xs, z, dvec_ref, nw_ref, wo_ref, xo_ref):
    y = y + dvec_ref[0] * xs
    y = y * _silu(z)
    y = _rms(y, nw_ref[...])
    xo_ref[...] += _dot(y.astype(BF16), wo_ref[0])


def _intra(cb, acs, e, acs_row, dt_row, mask):
    seg = acs[:, e:e + 1] - acs_row
    if mask is not None:
        seg = jnp.where(mask, seg, -jnp.inf)
    return (cb * jnp.exp(seg) * dt_row).astype(BF16)


def _head_row(t_scr, g, e, c0, n):
    return t_scr[pl.ds(HEADS_PER_GROUP * g + e, 1), c0:c0 + n]


def _ssd_stage2_prompt(slot, g, tm, pz_scr, pxbc_scr, hd, carry, state, cw, cb,
                       dvec_ref, nw_ref, wo_ref, xo_ref, cst_ref, hfin_ref, fills):
    q = SSD_Q
    half = q // 2
    fills = list(fills)
    acs_scr, eacs_scr, dend_scr, etot_scr, acs_t_scr, dt_t_scr = hd
    if fills:
        fills.pop(0)()
    pxbc_scr[slot, 0:HALO, :] = carry[g]
    acc = cb + pxbc_scr[slot, pl.ds(HALO - 3, tm), :] * cw[0:1]
    for k in (1, 2, 3):
        acc = acc + pxbc_scr[slot, pl.ds(HALO - 3 + k, tm), :] * cw[k:k + 1]
    last = pxbc_scr[slot, pl.ds(tm, HALO), :]
    carry[g] = last
    cst_ref[0, 0] = last
    xbc = _silu(acc)
    xs = xbc[:, :GROUP_W]
    xsb = xs.astype(BF16)
    bb = xbc[:, GROUP_W:GROUP_W + D_STATE].astype(BF16)
    cc = xbc[:, GROUP_W + D_STATE:].astype(BF16)
    ii = lax.broadcasted_iota(jnp.int32, (half, half), 0)
    jj = lax.broadcasted_iota(jnp.int32, (half, half), 1)
    tri = ii >= jj
    lane_head = lax.broadcasted_iota(jnp.int32, (q, GROUP_W), 1) >> 6
    hst = state[g]
    ys = []
    for c in range(tm // q):
        if fills:
            fills.pop(0)()
        r0 = c * q
        sl = slice(r0, r0 + q)
        acs = _group_lanes(acs_scr[sl], g)
        scale = _head_cols(_group_lanes(eacs_scr[sl], g), GROUP_W)
        dend = _head_cols(_group_lanes(dend_scr[sl], g), GROUP_W)
        etot = _group_lanes(etot_scr[r0:r0 + HALO], g)[0:1]
        cbm = _dot_nt(cc[sl], bb[sl])
        top, bot = None, None
        for e in range(HEADS_PER_GROUP):
            arow = _head_row(acs_t_scr, g, e, r0, q)
            drow = _head_row(dt_t_scr, g, e, r0, q)
            xe = jnp.where(lane_head == e, xsb[sl], jnp.zeros((q, GROUP_W), BF16))
            m00 = _intra(cbm[:half, :half], acs[:half], e, arow[:, :half], drow[:, :half], tri)
            m10 = _intra(cbm[half:, :half], acs[half:], e, arow[:, :half], drow[:, :half], None)
            m11 = _intra(cbm[half:, half:], acs[half:], e, arow[:, half:], drow[:, half:], tri)
            d0 = _dot(m00, xe[:half])
            d1 = _dot(jnp.concatenate([m10, m11], axis=1), xe)
            top = d0 if top is None else top + d0
            bot = d1 if bot is None else bot + d1
        ydiag = jnp.concatenate([top, bot], axis=0)
        yoff = _dot_nt(cc[sl], hst.astype(BF16)) * scale
        ys.append(ydiag + yoff)
        s_new = _dot_tn((xs[sl] * dend).astype(BF16), bb[sl])
        hst = hst * _head_rows(etot) + s_new
    for fill in fills:
        fill()
    state[g] = hst
    hfin_ref[0] = hst.reshape(HEADS_PER_GROUP, SSD_HEAD_DIM, D_STATE)
    y = ys[0] if len(ys) == 1 else jnp.concatenate(ys, axis=0)
    _ssd_finish(y, xs, pz_scr[slot], dvec_ref, nw_ref, wo_ref, xo_ref)


def _ssd_prompt_kernel(h_ref, xg_ref, wz_ref, wx_ref, wb_ref, wc_ref, wdt_ref, dtb_ref, alog_ref,
                       cwx_ref, cwb_ref, cwc_ref, cbx_ref, cbb_ref, cbc_ref, dvec_ref, nw_ref, wo_ref,
                       xo_ref, cst_ref, hfin_ref,
                       pz_scr, pxbc_scr, *scr):
    hd, (carry, state) = scr[:6], scr[6:]
    t = pl.program_id(1)
    s = pl.program_id(2)
    tm = h_ref.shape[0]

    def stage1(slot):
        return _ssd_stage1(h_ref, wz_ref, wx_ref, wb_ref, wc_ref, pz_scr, pxbc_scr, slot)

    def stage2(slot, g, fills):
        cw, cb = _conv_weights(cwx_ref, cwb_ref, cwc_ref, cbx_ref, cbb_ref, cbc_ref)
        _ssd_stage2_prompt(slot, g, tm, pz_scr, pxbc_scr, hd, carry, state, cw, cb,
                           dvec_ref, nw_ref, wo_ref, xo_ref, cst_ref, hfin_ref, fills)

    @pl.when((s == 0) & (t == 0))
    def _():
        carry[...] = jnp.zeros(carry.shape, F32)
        state[...] = jnp.zeros(state.shape, F32)

    @pl.when(s == 0)
    def _():
        xo_ref[...] = xg_ref[...]
        _ssd_heads(h_ref, wdt_ref, dtb_ref, alog_ref, SSD_Q, hd)
        for piece in stage1(0):
            piece()

    @pl.when((s > 0) & (s < SSD_GROUPS))
    def _():
        stage2((s - 1) & 1, s - 1, stage1(s & 1))

    @pl.when(s == SSD_GROUPS)
    def _():
        stage2((SSD_GROUPS - 1) & 1, SSD_GROUPS - 1, [])


def _ssd_sample_kernel(h_ref, xg_ref, wz_ref, wx_ref, wb_ref, wc_ref, wdt_ref, dtb_ref, alog_ref,
                       cwx_ref, cwb_ref, cwc_ref, cbx_ref, cbb_ref, cbc_ref, dvec_ref, nw_ref, wo_ref,
                       halo_ref, h0_ref, xo_ref, cst_ref, hfin_ref,
                       pz_scr, pxbc_scr, *hd, n_seq):
    acs_scr, eacs_scr, dend_scr, etot_scr, acs_t_scr, dt_t_scr = hd
    g = pl.program_id(1)
    tm = h_ref.shape[0]
    lt = tm // n_seq
    lt_shift = lt.bit_length() - 1

    @pl.when(g == 0)
    def _():
        xo_ref[...] = xg_ref[...]
        _ssd_heads(h_ref, wdt_ref, dtb_ref, alog_ref, lt, hd)

    for piece in _ssd_stage1(h_ref, wz_ref, wx_ref, wb_ref, wc_ref, pz_scr, pxbc_scr, 0):
        piece()
    cw, cb = _conv_weights(cwx_ref, cwb_ref, cwc_ref, cbx_ref, cbb_ref, cbc_ref)
    pre3 = pxbc_scr[0, pl.ds(HALO, tm), :].reshape(n_seq, lt, XBC_G)
    xp = jnp.concatenate([halo_ref[0], pre3], axis=1)
    acc = cb + xp[:, HALO - 3:HALO - 3 + lt] * cw[0:1]
    for k in (1, 2, 3):
        acc = acc + xp[:, HALO - 3 + k:HALO - 3 + k + lt] * cw[k:k + 1]
    cst_ref[0] = pre3[:, lt - HALO:]
    xbc = _silu(acc.reshape(tm, XBC_G))
    xs = xbc[:, :GROUP_W]
    xsb = xs.astype(BF16)
    bb = xbc[:, GROUP_W:GROUP_W + D_STATE].astype(BF16)
    cc = xbc[:, GROUP_W + D_STATE:].astype(BF16)
    ii = lax.broadcasted_iota(jnp.int32, (tm, tm), 0)
    jj = lax.broadcasted_iota(jnp.int32, (tm, tm), 1)
    mask = ((ii >> lt_shift) == (jj >> lt_shift)) & (ii >= jj)
    acs = _group_lanes(acs_scr[...], g)
    scale = _head_cols(_group_lanes(eacs_scr[...], g), GROUP_W)
    dend = _head_cols(_group_lanes(dend_scr[...], g), GROUP_W)
    cd = _group_lanes(etot_scr[...], g)
    cbm = _dot_nt(cc, bb)
    lane_head = lax.broadcasted_iota(jnp.int32, (tm, GROUP_W), 1) >> 6
    ydiag = None
    for e in range(HEADS_PER_GROUP):
        mp = _intra(cbm, acs, e, _head_row(acs_t_scr, g, e, 0, tm), _head_row(dt_t_scr, g, e, 0, tm), mask)
        d = _dot(mp, jnp.where(lane_head == e, xsb, jnp.zeros((tm, GROUP_W), BF16)))
        ydiag = d if ydiag is None else ydiag + d
    xdtd = (xs * dend).astype(BF16)
    row_seq = lax.broadcasted_iota(jnp.int32, (tm, D_STATE), 0) >> lt_shift
    zero = jnp.zeros((tm, D_STATE), BF16)
    h0 = [h0_ref[i].reshape(GROUP_W, D_STATE) for i in range(n_seq)]
    h0_wide = jnp.concatenate(h0, axis=1).astype(BF16)
    c_wide = jnp.concatenate([jnp.where(row_seq == i, cc, zero) for i in range(n_seq)], axis=1)
    b_wide = jnp.concatenate([jnp.where(row_seq == i, bb, zero) for i in range(n_seq)], axis=1)
    yoff = _dot_nt(c_wide, h0_wide) * scale
    s_new = _dot_tn(xdtd, b_wide)
    for i in range(n_seq):
        hn = h0[i] * _head_rows(cd[i * lt:i * lt + 1, :]) + s_new[:, i * D_STATE:(i + 1) * D_STATE]
        hfin_ref[i] = hn.reshape(HEADS_PER_GROUP, SSD_HEAD_DIM, D_STATE)
    _ssd_finish(ydiag + yoff, xs, pz_scr[0], dvec_ref, nw_ref, wo_ref, xo_ref)


def _ssd_weight_specs(g1, g2):
    col = lambda width, base, gf: pl.BlockSpec((D_MODEL, width), lambda *a: (0, base + gf(*a)))
    cv = lambda rows, width, base: pl.BlockSpec((rows, width), lambda *a: (0, base + g2(*a)))
    zero = lambda *a: 0
    return [
        col(GROUP_W, 0, g1), col(GROUP_W, S_XBC // GROUP_W, g1),
        col(D_STATE, S_B // D_STATE, g1), col(D_STATE, S_C // D_STATE, g1),
        col(LANES, S_DT // LANES, zero),
        pl.BlockSpec((1, LANES), lambda *a: (0, 0)), pl.BlockSpec((1, LANES), lambda *a: (0, 0)),
        cv(SSD_CONV, GROUP_W, 0), cv(SSD_CONV, D_STATE, D_SSD // D_STATE),
        cv(SSD_CONV, D_STATE, D_SSD // D_STATE + SSD_GROUPS),
        cv(1, GROUP_W, 0), cv(1, D_STATE, D_SSD // D_STATE), cv(1, D_STATE, D_SSD // D_STATE + SSD_GROUPS),
        pl.BlockSpec((1, 1, GROUP_W), lambda *a: (g2(*a), 0, 0)),
        cv(1, GROUP_W, 0),
        pl.BlockSpec((1, GROUP_W, D_MODEL), lambda *a: (g2(*a), 0, 0)),
    ]


def _ssd_weight_args(w):
    return [w["w_a"]] * 5 + [w["dtb"], w["alog"]] + [w["cw"]] * 3 + [w["cb"]] * 3 + [w["dvec"], w["nw"], w["wo"]]


def _ssd_scratch(tm, slots):
    return ([pltpu.VMEM((slots, tm, GROUP_W), F32), pltpu.VMEM((slots, HALO + tm, XBC_G), F32)]
            + [pltpu.VMEM((tm, LANES), F32)] * 4 + [pltpu.VMEM((LANES, tm), F32)] * 2)


def _ssd_prompt_call(h, xg, w, *, batch, tm):
    m = h.shape[0]
    nt = m // (batch * tm)
    row = lambda b, t, s: (b * nt + t, 0)
    g1 = lambda b, t, s: jnp.minimum(s, SSD_GROUPS - 1)
    g2 = lambda b, t, s: jnp.maximum(s - 1, 0)
    return pl.pallas_call(
        _ssd_prompt_kernel,
        out_shape=(jax.ShapeDtypeStruct((m, D_MODEL), F32),
                   jax.ShapeDtypeStruct((batch * nt, SSD_GROUPS, HALO, XBC_G), F32),
                   jax.ShapeDtypeStruct((batch * nt, SSD_HEADS, SSD_HEAD_DIM, D_STATE), F32)),
        grid=(batch, nt, SSD_GROUPS + 1),
        in_specs=[pl.BlockSpec((tm, D_MODEL), row), pl.BlockSpec((tm, D_MODEL), row)] + _ssd_weight_specs(g1, g2),
        out_specs=(
            pl.BlockSpec((tm, D_MODEL), row),
            pl.BlockSpec((1, 1, HALO, XBC_G), lambda b, t, s: (b * nt + t, g2(b, t, s), 0, 0)),
            pl.BlockSpec((1, HEADS_PER_GROUP, SSD_HEAD_DIM, D_STATE),
                         lambda b, t, s: (b * nt + t, g2(b, t, s), 0, 0)),
        ),
        scratch_shapes=_ssd_scratch(tm, 2) + [pltpu.VMEM((SSD_GROUPS, HALO, XBC_G), F32),
                                              pltpu.VMEM((SSD_GROUPS, GROUP_W, D_STATE), F32)],
        compiler_params=pltpu.CompilerParams(
            dimension_semantics=("arbitrary", "arbitrary", "arbitrary"), vmem_limit_bytes=VMEM_LIMIT),
        name="ssd_prompt_group",
    )(h, xg, *_ssd_weight_args(w))


def _ssd_sample_call(h, xg, w, halo, h0, *, n_seq, lt):
    m = h.shape[0]
    tm = n_seq * lt
    nt = m // tm
    n_all = m // lt
    kern = functools.partial(_ssd_sample_kernel, n_seq=n_seq)
    row = lambda i, g: (i, 0)
    gi = lambda i, g: g
    return pl.pallas_call(
        kern,
        out_shape=(jax.ShapeDtypeStruct((m, D_MODEL), F32),
                   jax.ShapeDtypeStruct((SSD_GROUPS, n_all, HALO, XBC_G), F32),
                   jax.ShapeDtypeStruct((n_all, SSD_HEADS, SSD_HEAD_DIM, D_STATE), F32)),
        grid=(nt, SSD_GROUPS),
        in_specs=[pl.BlockSpec((tm, D_MODEL), row), pl.BlockSpec((tm, D_MODEL), row)] + _ssd_weight_specs(gi, gi)
        + [pl.BlockSpec((1, n_seq, HALO, XBC_G), lambda i, g: (g, i, 0, 0)),
           pl.BlockSpec((n_seq, HEADS_PER_GROUP, SSD_HEAD_DIM, D_STATE), lambda i, g: (i, g, 0, 0))],
        out_specs=(
            pl.BlockSpec((tm, D_MODEL), row),
            pl.BlockSpec((1, n_seq, HALO, XBC_G), lambda i, g: (g, i, 0, 0)),
            pl.BlockSpec((n_seq, HEADS_PER_GROUP, SSD_HEAD_DIM, D_STATE), lambda i, g: (i, g, 0, 0)),
        ),
        scratch_shapes=_ssd_scratch(tm, 1),
        compiler_params=pltpu.CompilerParams(
            dimension_semantics=("arbitrary", "arbitrary"), vmem_limit_bytes=VMEM_LIMIT),
        name="ssd_sample_group",
    )(h, xg, *_ssd_weight_args(w), halo, h0)


def _ffn_kernel(*refs, n_seq, carried, final):
    if carried:
        (x_ref, n2_ref, wg_ref, wv_ref, cwg_ref, cwv_ref, cbg_ref, cbv_ref, wd_ref, fn_ref,
         xo_ref, lg_ref, lv_ref, h_scr, ug_scr, uv_scr, carry_g, carry_v) = refs
    else:
        (x_ref, n2_ref, wg_ref, wv_ref, cwg_ref, cwv_ref, cbg_ref, cbv_ref, wd_ref, fn_ref, hg_ref, hv_ref,
         xo_ref, lg_ref, lv_ref, h_scr, ug_scr, uv_scr) = refs
    nax = 3 if carried else 2
    s = pl.program_id(nax - 1)
    tm = x_ref.shape[0]
    lt = tm // n_seq
    tf = wg_ref.shape[1]

    def up_g(slot):
        ug_scr[slot] = _dot(h_scr[...], wg_ref[...])

    def up_v(slot):
        uv_scr[slot] = _dot(h_scr[...], wv_ref[...])

    def conv(u, halo, cw, cb):
        if n_seq == 1:
            xp = jnp.concatenate([halo[0], u], axis=0)
            acc = cb + xp[HALO - 2:HALO - 2 + tm] * cw[0:1]
            acc = acc + xp[HALO - 1:HALO - 1 + tm] * cw[1:2]
            acc = acc + xp[HALO:] * cw[2:3]
            return acc, u[tm - HALO:].reshape(1, HALO, tf)
        u3 = u.reshape(n_seq, lt, tf)
        xp = jnp.concatenate([halo, u3], axis=1)
        acc = cb + xp[:, HALO - 2:HALO - 2 + lt] * cw[0:1]
        acc = acc + xp[:, HALO - 1:HALO - 1 + lt] * cw[1:2]
        acc = acc + xp[:, HALO:] * cw[2:3]
        return acc.reshape(tm, tf), u3[:, lt - HALO:]

    def gate(slot, f, fills):
        fills = list(fills)
        if fills:
            fills.pop(0)()
        if carried:
            halo_g = carry_g[f].reshape(1, HALO, tf)
            halo_v = carry_v[f].reshape(1, HALO, tf)
        else:
            halo_g = hg_ref[...]
            halo_v = hv_ref[...]
        cg, last_g = conv(ug_scr[slot], halo_g, cwg_ref[...], cbg_ref[...])
        if fills:
            fills.pop(0)()
        cv, last_v = conv(uv_scr[slot], halo_v, cwv_ref[...], cbv_ref[...])
        lg_ref[...] = last_g
        lv_ref[...] = last_v
        if carried:
            carry_g[f] = last_g[0]
            carry_v[f] = last_v[0]
        act = (_silu(cg) * cv).astype(BF16)
        xo_ref[...] += _dot(act, wd_ref[...])

    if carried:
        @pl.when((s == 0) & (pl.program_id(1) == 0))
        def _():
            carry_g[...] = jnp.zeros(carry_g.shape, F32)
            carry_v[...] = jnp.zeros(carry_v.shape, F32)

    @pl.when(s == 0)
    def _():
        x = x_ref[...]
        h_scr[...] = _rms(x, n2_ref[...]).astype(BF16)
        xo_ref[...] = x
        up_g(0)
        up_v(0)

    @pl.when((s > 0) & (s < NF))
    def _():
        slot = s & 1
        gate((s - 1) & 1, s - 1, [functools.partial(up_g, slot), functools.partial(up_v, slot)])

    @pl.when(s == NF)
    def _():
        gate((NF - 1) & 1, NF - 1, [])
        if final:
            xo_ref[...] = _rms(xo_ref[...], fn_ref[...])


def _ffn_call(x, n2, w_up, cw, cb, w_down, fnw, halos, *, batch, tm, n_seq, final):
    m = x.shape[0]
    carried = halos is None
    kern = functools.partial(_ffn_kernel, n_seq=n_seq, carried=carried, final=final)
    if carried:
        nt = m // (batch * tm)
        grid = (batch, nt, NF + 1)
        row = lambda b, t, s: (b * nt + t, 0)
        step = lambda b, t, s: s
        si = lambda b, t, s: b * nt + t
        n_state = batch * nt
    else:
        grid = (m // tm, NF + 1)
        row = lambda i, s: (i, 0)
        step = lambda i, s: s
        si = lambda i, s: i
        n_state = m // (tm // n_seq)
    f1 = lambda *a: jnp.minimum(step(*a), NF - 1)
    fi = lambda *a: jnp.maximum(step(*a) - 1, 0)
    col = lambda fn, off: (lambda *a: (0, fn(*a) + off))
    in_specs = [
        pl.BlockSpec((tm, D_MODEL), row),
        pl.BlockSpec((1, D_MODEL), lambda *a: (0, 0)),
        pl.BlockSpec((D_MODEL, TF), col(f1, 0)),
        pl.BlockSpec((D_MODEL, TF), col(f1, NF)),
        pl.BlockSpec((FFN_CONV, TF), col(fi, 0)),
        pl.BlockSpec((FFN_CONV, TF), col(fi, NF)),
        pl.BlockSpec((1, TF), col(fi, 0)),
        pl.BlockSpec((1, TF), col(fi, NF)),
        pl.BlockSpec((TF, D_MODEL), lambda *a: (fi(*a), 0)),
        pl.BlockSpec((1, D_MODEL), lambda *a: (0, 0)),
    ]
    args = [x, n2, w_up, w_up, cw, cw, cb, cb, w_down, fnw]
    scratch = [pltpu.VMEM((tm, D_MODEL), BF16), pltpu.VMEM((2, tm, TF), F32), pltpu.VMEM((2, tm, TF), F32)]
    if carried:
        scratch += [pltpu.VMEM((NF, HALO, TF), F32), pltpu.VMEM((NF, HALO, TF), F32)]
    else:
        in_specs += [pl.BlockSpec((n_seq, HALO, TF), lambda *a: (si(*a), 0, fi(*a))),
                     pl.BlockSpec((n_seq, HALO, TF), lambda *a: (si(*a), 0, fi(*a) + NF))]
        args += [halos, halos]
    st_rows = 1 if carried else n_seq
    return pl.pallas_call(
        kern,
        out_shape=(jax.ShapeDtypeStruct((m, D_MODEL), F32),
                   jax.ShapeDtypeStruct((n_state, HALO, D_FF), F32),
                   jax.ShapeDtypeStruct((n_state, HALO, D_FF), F32)),
        grid=grid,
        in_specs=in_specs,
        out_specs=(
            pl.BlockSpec((tm, D_MODEL), row),
            pl.BlockSpec((st_rows, HALO, TF), lambda *a: (si(*a), 0, fi(*a))),
            pl.BlockSpec((st_rows, HALO, TF), lambda *a: (si(*a), 0, fi(*a))),
        ),
        scratch_shapes=scratch,
        compiler_params=pltpu.CompilerParams(
            dimension_semantics=("arbitrary",) * len(grid), vmem_limit_bytes=VMEM_LIMIT),
        name="ffn_prompt" if carried else "ffn_sample",
    )(*args)


def _group_cols(w, n_groups):
    return w.reshape(w.shape[:-1] + (n_groups, w.shape[-1] // n_groups))


def _xbc_to_groups(a):
    xs = _group_cols(a[..., :D_SSD], SSD_GROUPS)
    bs = _group_cols(a[..., D_SSD:D_SSD + SSD_GROUPS * D_STATE], SSD_GROUPS)
    cs = _group_cols(a[..., D_SSD + SSD_GROUPS * D_STATE:], SSD_GROUPS)
    return jnp.concatenate([xs, bs, cs], axis=-1)


def _groups_to_xbc(a):
    lead = a.shape[:-2]
    xs = a[..., :GROUP_W].reshape(lead + (D_SSD,))
    bs = a[..., GROUP_W:GROUP_W + D_STATE].reshape(lead + (SSD_GROUPS * D_STATE,))
    cs = a[..., GROUP_W + D_STATE:].reshape(lead + (SSD_GROUPS * D_STATE,))
    return jnp.concatenate([xs, bs, cs], axis=-1)


def _pad_lanes(v):
    return jnp.pad(v, (0, LANES - v.shape[0]))[None, :]


def _pad_halo(state):
    return jnp.pad(state, ((0, 0), (HALO - state.shape[1], 0), (0, 0)))


def kernel(x_prompt, x_sample, state_ssd_conv, state_ssd, state_ffn_conv, norm1_w, w_in, ssd_conv_w, ssd_conv_b,
           dt_bias, a_log, ssd_d, ssd_norm_w, gmlp_norm_w, gmlp_w_s, gmlp_b_s, w_out, norm2_w, w_up, ffn_conv_w,
           ffn_conv_b, w_down, final_norm_w):
    bp, lp, _ = x_prompt.shape
    bs, ls, _ = x_sample.shape
    xp = x_prompt.reshape(bp * lp, D_MODEL)
    xs = x_sample.reshape(bs * ls, D_MODEL)
    ms = bs * ls
    fnw = final_norm_w[None, :]
    ssd_seqs = 16
    ntp = lp // TM
    p_conv, p_ssd, p_ffn = [], [], []
    s_conv, s_ssd, s_ffn, s_v = [], [], [], []
    for l in range(DEPTH):
        wo = w_out[l].astype(BF16).reshape(2 * SSD_GROUPS, GROUP_W, D_MODEL)
        ssd_w = dict(
            w_a=w_in[l][:, :W_SSD_COLS].astype(BF16), dtb=_pad_lanes(dt_bias[l]), alog=_pad_lanes(a_log[l]),
            cw=ssd_conv_w[l], cb=ssd_conv_b[l][None, :],
            dvec=jnp.repeat(ssd_d[l], SSD_HEAD_DIM).reshape(SSD_GROUPS, 1, GROUP_W),
            nw=ssd_norm_w[l][None, :], wo=wo)
        w_uv = w_in[l][:, S_U:].astype(BF16)
        n1 = norm1_w[l][None, :]
        gnw = gmlp_norm_w[l][None, :]
        n2 = norm2_w[l][None, :]
        wu = w_up[l].astype(BF16)
        wd = w_down[l].astype(BF16)
        fcw = ffn_conv_w[l]
        fcb = ffn_conv_b[l][None, :]
        final = l == DEPTH - 1
        xg, h = _gmlp_call(xp, n1, w_uv, gnw, gmlp_w_s[l], gmlp_b_s[l][:, :, None], wo,
                           tm=TM, chunk=GMLP_CHUNK, seg_len=GMLP_CHUNK, emit_vn=False)
        x1, cst, hfin = _ssd_prompt_call(h, xg, ssd_w, batch=bp, tm=TM)
        xp, lg, lv = _ffn_call(x1, n2, wu, fcw, fcb, wd, fnw, None, batch=bp, tm=TM, n_seq=1, final=final)
        cst, hfin, lg, lv = (a.reshape((bp, ntp) + a.shape[1:])[:, -1] for a in (cst, hfin, lg, lv))
        p_conv.append(_groups_to_xbc(cst[:, :, HALO - (SSD_CONV - 1):, :].transpose(0, 2, 1, 3)))
        p_ssd.append(hfin)
        p_ffn.append(jnp.concatenate([lg, lv], axis=-1)[:, HALO - (FFN_CONV - 1):, :])
        ws_t = jnp.tile(gmlp_w_s[l][:, :ls, :ls], (1, 1, bs))
        bs_t = jnp.tile(gmlp_b_s[l][:, :ls], (1, bs))[:, :, None]
        xg, h, vn = _gmlp_call(xs, n1, w_uv, gnw, ws_t, bs_t, wo, tm=ms, chunk=ms, seg_len=ls, emit_vn=True)
        halo = _pad_halo(_xbc_to_groups(state_ssd_conv[l]).transpose(2, 0, 1, 3).reshape(
            SSD_GROUPS * bs, SSD_CONV - 1, XBC_G)).reshape(SSD_GROUPS, bs, HALO, XBC_G)
        x1, cst, hfin = _ssd_sample_call(h, xg, ssd_w, halo, state_ssd[l], n_seq=ssd_seqs, lt=ls)
        xs, lg, lv = _ffn_call(x1, n2, wu, fcw, fcb, wd, fnw, _pad_halo(state_ffn_conv[l]),
                               batch=1, tm=ms, n_seq=bs, final=final)
        s_conv.append(_groups_to_xbc(cst[:, :, HALO - (SSD_CONV - 1):, :].transpose(1, 2, 0, 3)))
        s_ssd.append(hfin)
        s_ffn.append(jnp.concatenate([lg, lv], axis=-1)[:, HALO - (FFN_CONV - 1):, :])
        s_v.append(vn.reshape(bs, ls, D_GMLP))
    return (xp.reshape(bp, lp, D_MODEL), xs.reshape(bs, ls, D_MODEL),
            jnp.stack(p_conv), jnp.stack(p_ssd), jnp.stack(p_ffn),
            jnp.stack(s_conv), jnp.stack(s_ssd), jnp.stack(s_ffn), jnp.stack(s_v))
```

```python
import functools

import jax
import jax.numpy as jnp
from jax import lax
from jax.experimental import pallas as pl
from jax.experimental.pallas import tpu as pltpu

D_MODEL = 2048
DEPTH = 2
CHUNK = 64
D_SSD = 2048
SSD_HEAD_DIM = 64
SSD_HEADS = 32
SSD_GROUPS = 8
HEADS_PER_GROUP = 4
D_STATE = 128
SSD_CONV = 4
D_XBC = 4096
D_GMLP = 2048
GMLP_GROUPS = 8
GMLP_CHUNK = 128
D_FF = 5632
FFN_CONV = 3
EPS = 1e-6

LANES = 128
GROUP_W = 256
XBC_G = GROUP_W + 2 * D_STATE
SSD_COLS = GROUP_W + XBC_G
HALO = 8
TM = 512
SSD_Q = 256
TF = 512
NF = D_FF // TF
VMEM_LIMIT = 56 * 1024 * 1024

S_X = D_SSD
S_B = S_X + D_SSD
S_C = S_B + SSD_GROUPS * D_STATE
S_DT = S_C + SSD_GROUPS * D_STATE
S_U = S_DT + SSD_HEADS
S_V = S_U + D_GMLP

F32 = jnp.float32
BF16 = jnp.bfloat16


def _dot(a, b):
    return jnp.dot(a, b, preferred_element_type=F32)


def _dot_nt(a, b):
    return lax.dot_general(a, b, (((1,), (1,)), ((), ())), preferred_element_type=F32)


def _dot_tn(a, b):
    return lax.dot_general(a, b, (((0,), (0,)), ((), ())), preferred_element_type=F32)


def _rms(x, w):
    return x * lax.rsqrt(jnp.mean(x * x, axis=-1, keepdims=True) + EPS) * w


def _silu(x):
    return x * jax.nn.sigmoid(x)


def _softplus(x):
    return jnp.maximum(x, 0.0) + jnp.log1p(jnp.exp(-jnp.abs(x)))


def _seg_cumsum(x, seg_len):
    pos = lax.broadcasted_iota(jnp.int32, x.shape, 0) & (seg_len - 1)
    k = 1
    while k < seg_len:
        x = x + jnp.where(pos >= k, pltpu.roll(x, k, 0), 0.0)
        k *= 2
    return x


def _seg_total(x, seg_len):
    n = x.shape[0]
    pos = lax.broadcasted_iota(jnp.int32, x.shape, 0) & (seg_len - 1)
    k = 1
    while k < seg_len:
        x = x + jnp.where((pos & k) == 0, pltpu.roll(x, n - k, 0), pltpu.roll(x, k, 0))
        k *= 2
    return x


def _head_cols(v, width):
    n = v.shape[0]
    lane_head = lax.broadcasted_iota(jnp.int32, (n, width), 1) >> 6
    out = jnp.broadcast_to(v[:, 3:4], (n, width))
    for e in (2, 1, 0):
        out = jnp.where(lane_head == e, jnp.broadcast_to(v[:, e:e + 1], (n, width)), out)
    return out


def _head_rows(row):
    row_head = lax.broadcasted_iota(jnp.int32, (GROUP_W, D_STATE), 0) >> 6
    out = jnp.broadcast_to(row[:, 3:4], (GROUP_W, D_STATE))
    for e in (2, 1, 0):
        out = jnp.where(row_head == e, jnp.broadcast_to(row[:, e:e + 1], (GROUP_W, D_STATE)), out)
    return out


def _run_skewed(s, n, first, stage1, stage2):
    @pl.when(s == 0)
    def _():
        first()
        for piece in stage1(0):
            piece()

    @pl.when((s > 0) & (s < n))
    def _():
        stage2((s - 1) & 1, s - 1, stage1(s & 1))

    @pl.when(s == n)
    def _():
        stage2((n - 1) & 1, n - 1, [])


def _gmlp_kernel(x_ref, n1_ref, w_ref, gnw_ref, ws_ref, bs_ref, wo_ref, xo_ref, h_ref, *rest,
                 chunk, seg_len, emit_vn):
    vn_ref = rest[0] if emit_vn else None
    h_scr, u_scr, v_scr = rest[-3:]
    s = pl.program_id(1)
    tm = x_ref.shape[0]
    seg_shift = seg_len.bit_length() - 1
    chunk_shift = CHUNK.bit_length() - 1

    def first():
        x = x_ref[...]
        hb = _rms(x, n1_ref[0]).astype(BF16)
        h_scr[...] = hb
        h_ref[...] = hb
        xo_ref[...] = x

    def project(slot):
        def proj_u():
            u_scr[slot] = _dot(h_scr[...], w_ref[0, 0, :, 0:GROUP_W])

        def proj_v():
            v_scr[slot] = _dot(h_scr[...], w_ref[0, 0, :, GROUP_W:2 * GROUP_W])

        return [proj_u, proj_v]

    def gate(slot, g, fills):
        fills = list(fills)
        if fills:
            fills.pop(0)()
        u = jax.nn.gelu(u_scr[slot])
        v = jax.nn.gelu(v_scr[slot])
        vn = _rms(v, gnw_ref[0, g])
        if emit_vn:
            vn_ref[...] = vn
        for fill in fills:
            fill()
        ii = lax.broadcasted_iota(jnp.int32, (chunk, chunk), 0)
        jj = lax.broadcasted_iota(jnp.int32, (chunk, chunk), 1)
        same_seq = (ii >> seg_shift) == (jj >> seg_shift)
        block_causal = ((jj & (seg_len - 1)) >> chunk_shift) <= ((ii & (seg_len - 1)) >> chunk_shift)
        w_rows = ws_ref[0, 0]
        w_full = w_rows if seg_len == chunk else jnp.concatenate([w_rows] * (chunk // seg_len), axis=0)
        wm = jnp.where(same_seq & block_causal, w_full, 0.0).astype(BF16)
        vnb = vn.astype(BF16)
        bias = bs_ref[0, g]
        parts = []
        for c in range(tm // chunk):
            sg = _dot(wm, vnb[c * chunk:(c + 1) * chunk]) + bias
            parts.append(u[c * chunk:(c + 1) * chunk] * sg)
        y = parts[0] if len(parts) == 1 else jnp.concatenate(parts, axis=0)
        xo_ref[...] += _dot(y.astype(BF16), wo_ref[0, 0])

    _run_skewed(s, GMLP_GROUPS, first, project, gate)


def _gmlp_call(l, x, w, ws, bs, *, tm, chunk, seg_len, emit_vn):
    m = x.shape[0]
    nt = m // tm
    kern = functools.partial(_gmlp_kernel, chunk=chunk, seg_len=seg_len, emit_vn=emit_vn)
    out_shape = [jax.ShapeDtypeStruct((m, D_MODEL), F32), jax.ShapeDtypeStruct((m, D_MODEL), BF16)]
    out_specs = [pl.BlockSpec((tm, D_MODEL), lambda i, s: (i, 0)), pl.BlockSpec((tm, D_MODEL), lambda i, s: (i, 0))]
    g1 = lambda s: jnp.minimum(s, GMLP_GROUPS - 1)
    g2 = lambda s: jnp.maximum(s - 1, 0)
    if emit_vn:
        out_shape.append(jax.ShapeDtypeStruct((m, D_GMLP), F32))
        out_specs.append(pl.BlockSpec((tm, GROUP_W), lambda i, s: (i, g2(s))))
    return pl.pallas_call(
        kern,
        out_shape=tuple(out_shape),
        grid=(nt, GMLP_GROUPS + 1),
        in_specs=[
            pl.BlockSpec((tm, D_MODEL), lambda i, s: (i, 0)),
            pl.BlockSpec((1, 1, D_MODEL), lambda i, s: (l, 0, 0)),
            pl.BlockSpec((1, 1, D_MODEL, 2 * GROUP_W), lambda i, s: (l, g1(s), 0, 0)),
            pl.BlockSpec((1, GMLP_GROUPS, 1, GROUP_W), lambda i, s: (l, 0, 0, 0)),
            pl.BlockSpec((1, 1, seg_len, chunk), lambda i, s: (l, g2(s), 0, 0)),
            pl.BlockSpec((1, GMLP_GROUPS, chunk, 1), lambda i, s: (l, 0, 0, 0)),
            pl.BlockSpec((1, 1, GROUP_W, D_MODEL), lambda i, s: (l, SSD_GROUPS + g2(s), 0, 0)),
        ],
        out_specs=tuple(out_specs),
        scratch_shapes=[pltpu.VMEM((tm, D_MODEL), BF16), pltpu.VMEM((2, tm, GROUP_W), F32),
                        pltpu.VMEM((2, tm, GROUP_W), F32)],
        compiler_params=pltpu.CompilerParams(
            dimension_semantics=("arbitrary", "arbitrary"), vmem_limit_bytes=VMEM_LIMIT),
        name="gmlp_group",
    )(x, w["n1"], w["w_gm"], w["gnw"], ws, bs, w["wo"])


def _ssd_heads(h_ref, wdt_ref, dtb_ref, alog_ref, seg_len, hd):
    acs_scr, eacs_scr, dend_scr, etot_scr, acs_t_scr, dt_t_scr = hd
    dt = _softplus(_dot(h_ref[...], wdt_ref[0]) + dtb_ref[0])
    dta = dt * (-jnp.exp(alog_ref[0]))
    acs = _seg_cumsum(dta, seg_len)
    tot = _seg_total(dta, seg_len)
    acs_scr[...] = acs
    eacs_scr[...] = jnp.exp(acs)
    dend_scr[...] = jnp.exp(tot - acs) * dt
    etot_scr[...] = jnp.exp(tot)
    acs_t_scr[...] = acs.T
    dt_t_scr[...] = dt.T


def _group_lanes(v, g):
    return pltpu.roll(v, (LANES - HEADS_PER_GROUP * g) & (LANES - 1), 1)


def _ssd_stage1(h_ref, w_ref, pz_scr, pxbc_scr, slot):
    tm = h_ref.shape[0]
    rows = pl.ds(HALO, tm)

    def proj_z():
        pz_scr[slot] = _dot(h_ref[...], w_ref[0, 0, :, 0:GROUP_W])

    def proj_x():
        pxbc_scr[slot, rows, 0:GROUP_W] = _dot(h_ref[...], w_ref[0, 0, :, GROUP_W:2 * GROUP_W])

    def proj_bc():
        pxbc_scr[slot, rows, GROUP_W:XBC_G] = _dot(h_ref[...], w_ref[0, 0, :, 2 * GROUP_W:SSD_COLS])

    return [proj_z, proj_x, proj_bc]


def _ssd_finish(g, y, xs, z, dvec_ref, nw_ref, wo_ref, xo_ref):
    y = y + dvec_ref[0, g] * xs
    y = y * _silu(z)
    y = _rms(y, nw_ref[0, g])
    xo_ref[...] += _dot(y.astype(BF16), wo_ref[0, 0])


def _intra(cb, acs, e, acs_row, dt_row, mask):
    seg = acs[:, e:e + 1] - acs_row
    if mask is not None:
        seg = jnp.where(mask, seg, -jnp.inf)
    return (cb * jnp.exp(seg) * dt_row).astype(BF16)


def _head_row(t_scr, g, e, c0, n):
    return t_scr[pl.ds(HEADS_PER_GROUP * g + e, 1), c0:c0 + n]


def _ssd_stage2_prompt(slot, g, tm, pz_scr, pxbc_scr, hd, carry, state, cw_ref, cb_ref,
                       dvec_ref, nw_ref, wo_ref, xo_ref, cst_ref, hfin_ref, fills):
    q = SSD_Q
    half = q // 2
    fills = list(fills)
    acs_scr, eacs_scr, dend_scr, etot_scr, acs_t_scr, dt_t_scr = hd
    if fills:
        fills.pop(0)()
    cw = cw_ref[0, g]
    pxbc_scr[slot, 0:HALO, :] = carry[g]
    acc = cb_ref[0, g] + pxbc_scr[slot, pl.ds(HALO - 3, tm), :] * cw[0:1]
    for k in (1, 2, 3):
        acc = acc + pxbc_scr[slot, pl.ds(HALO - 3 + k, tm), :] * cw[k:k + 1]
    last = pxbc_scr[slot, pl.ds(tm, HALO), :]
    carry[g] = last
    cst_ref[0, 0] = last
    xbc = _silu(acc)
    xs = xbc[:, :GROUP_W]
    xsb = xs.astype(BF16)
    bb = xbc[:, GROUP_W:GROUP_W + D_STATE].astype(BF16)
    cc = xbc[:, GROUP_W + D_STATE:].astype(BF16)
    ii = lax.broadcasted_iota(jnp.int32, (half, half), 0)
    jj = lax.broadcasted_iota(jnp.int32, (half, half), 1)
    tri = ii >= jj
    lane_head = lax.broadcasted_iota(jnp.int32, (q, GROUP_W), 1) >> 6
    hst = state[g]
    ys = []
    for c in range(tm // q):
        if fills:
            fills.pop(0)()
        r0 = c * q
        sl = slice(r0, r0 + q)
        acs = _group_lanes(acs_scr[sl], g)
        scale = _head_cols(_group_lanes(eacs_scr[sl], g), GROUP_W)
        dend = _head_cols(_group_lanes(dend_scr[sl], g), GROUP_W)
        etot = _group_lanes(etot_scr[r0:r0 + HALO], g)[0:1]
        cbm = _dot_nt(cc[sl], bb[sl])
        top, bot = None, None
        for e in range(HEADS_PER_GROUP):
            arow = _head_row(acs_t_scr, g, e, r0, q)
            drow = _head_row(dt_t_scr, g, e, r0, q)
            xe = jnp.where(lane_head == e, xsb[sl], jnp.zeros((q, GROUP_W), BF16))
            m00 = _intra(cbm[:half, :half], acs[:half], e, arow[:, :half], drow[:, :half], tri)
            m10 = _intra(cbm[half:, :half], acs[half:], e, arow[:, :half], drow[:, :half], None)
            m11 = _intra(cbm[half:, half:], acs[half:], e, arow[:, half:], drow[:, half:], tri)
            d0 = _dot(m00, xe[:half])
            d1 = _dot(jnp.concatenate([m10, m11], axis=1), xe)
            top = d0 if top is None else top + d0
            bot = d1 if bot is None else bot + d1
        ydiag = jnp.concatenate([top, bot], axis=0)
        yoff = _dot_nt(cc[sl], hst.astype(BF16)) * scale
        ys.append(ydiag + yoff)
        s_new = _dot_tn((xs[sl] * dend).astype(BF16), bb[sl])
        hst = hst * _head_rows(etot) + s_new
    for fill in fills:
        fill()
    state[g] = hst
    hfin_ref[0] = hst.reshape(HEADS_PER_GROUP, SSD_HEAD_DIM, D_STATE)
    y = ys[0] if len(ys) == 1 else jnp.concatenate(ys, axis=0)
    _ssd_finish(g, y, xs, pz_scr[slot], dvec_ref, nw_ref, wo_ref, xo_ref)


def _ssd_prompt_kernel(h_ref, xg_ref, w_ref, wdt_ref, dtb_ref, alog_ref, cw_ref, cb_ref, dvec_ref, nw_ref, wo_ref,
                       xo_ref, cst_ref, hfin_ref, pz_scr, pxbc_scr, *scr):
    hd, (carry, state) = scr[:6], scr[6:]
    t = pl.program_id(1)
    s = pl.program_id(2)
    tm = h_ref.shape[0]

    @pl.when((s == 0) & (t == 0))
    def _():
        carry[...] = jnp.zeros(carry.shape, F32)
        state[...] = jnp.zeros(state.shape, F32)

    def first():
        xo_ref[...] = xg_ref[...]
        _ssd_heads(h_ref, wdt_ref, dtb_ref, alog_ref, SSD_Q, hd)

    def stage1(slot):
        return _ssd_stage1(h_ref, w_ref, pz_scr, pxbc_scr, slot)

    def stage2(slot, g, fills):
        _ssd_stage2_prompt(slot, g, tm, pz_scr, pxbc_scr, hd, carry, state, cw_ref, cb_ref,
                           dvec_ref, nw_ref, wo_ref, xo_ref, cst_ref, hfin_ref, fills)

    _run_skewed(s, SSD_GROUPS, first, stage1, stage2)


def _ssd_sample_kernel(h_ref, xg_ref, w_ref, wdt_ref, dtb_ref, alog_ref, cw_ref, cb_ref, dvec_ref, nw_ref, wo_ref,
                       halo_ref, h0_ref, xo_ref, cst_ref, hfin_ref, pz_scr, pxbc_scr, *hd, n_seq):
    acs_scr, eacs_scr, dend_scr, etot_scr, acs_t_scr, dt_t_scr = hd
    g = pl.program_id(1)
    tm = h_ref.shape[0]
    lt = tm // n_seq
    lt_shift = lt.bit_length() - 1

    @pl.when(g == 0)
    def _():
        xo_ref[...] = xg_ref[...]
        _ssd_heads(h_ref, wdt_ref, dtb_ref, alog_ref, lt, hd)

    for piece in _ssd_stage1(h_ref, w_ref, pz_scr, pxbc_scr, 0):
        piece()
    cw = cw_ref[0, g]
    pre3 = pxbc_scr[0, pl.ds(HALO, tm), :].reshape(n_seq, lt, XBC_G)
    xp = jnp.concatenate([halo_ref[0], pre3], axis=1)
    acc = cb_ref[0, g] + xp[:, HALO - 3:HALO - 3 + lt] * cw[0:1]
    for k in (1, 2, 3):
        acc = acc + xp[:, HALO - 3 + k:HALO - 3 + k + lt] * cw[k:k + 1]
    cst_ref[0] = pre3[:, lt - HALO:]
    xbc = _silu(acc.reshape(tm, XBC_G))
    xs = xbc[:, :GROUP_W]
    xsb = xs.astype(BF16)
    bb = xbc[:, GROUP_W:GROUP_W + D_STATE].astype(BF16)
    cc = xbc[:, GROUP_W + D_STATE:].astype(BF16)
    ii = lax.broadcasted_iota(jnp.int32, (tm, tm), 0)
    jj = lax.broadcasted_iota(jnp.int32, (tm, tm), 1)
    mask = ((ii >> lt_shift) == (jj >> lt_shift)) & (ii >= jj)
    acs = _group_lanes(acs_scr[...], g)
    scale = _head_cols(_group_lanes(eacs_scr[...], g), GROUP_W)
    dend = _head_cols(_group_lanes(dend_scr[...], g), GROUP_W)
    cd = _group_lanes(etot_scr[...], g)
    cbm = _dot_nt(cc, bb)
    lane_head = lax.broadcasted_iota(jnp.int32, (tm, GROUP_W), 1) >> 6
    ydiag = None
    for e in range(HEADS_PER_GROUP):
        mp = _intra(cbm, acs, e, _head_row(acs_t_scr, g, e, 0, tm), _head_row(dt_t_scr, g, e, 0, tm), mask)
        d = _dot(mp, jnp.where(lane_head == e, xsb, jnp.zeros((tm, GROUP_W), BF16)))
        ydiag = d if ydiag is None else ydiag + d
    xdtd = (xs * dend).astype(BF16)
    row_seq = lax.broadcasted_iota(jnp.int32, (tm, D_STATE), 0) >> lt_shift
    zero = jnp.zeros((tm, D_STATE), BF16)
    h0 = [h0_ref[0, i].reshape(GROUP_W, D_STATE) for i in range(n_seq)]
    h0_wide = jnp.concatenate(h0, axis=1).astype(BF16)
    c_wide = jnp.concatenate([jnp.where(row_seq == i, cc, zero) for i in range(n_seq)], axis=1)
    b_wide = jnp.concatenate([jnp.where(row_seq == i, bb, zero) for i in range(n_seq)], axis=1)
    yoff = _dot_nt(c_wide, h0_wide) * scale
    s_new = _dot_tn(xdtd, b_wide)
    for i in range(n_seq):
        hn = h0[i] * _head_rows(cd[i * lt:i * lt + 1, :]) + s_new[:, i * D_STATE:(i + 1) * D_STATE]
        hfin_ref[i] = hn.reshape(HEADS_PER_GROUP, SSD_HEAD_DIM, D_STATE)
    _ssd_finish(g, ydiag + yoff, xs, pz_scr[0], dvec_ref, nw_ref, wo_ref, xo_ref)


def _ssd_weight_specs(l, g1, g2):
    whole = lambda rows, width: pl.BlockSpec((1, SSD_GROUPS, rows, width), lambda *a: (l, 0, 0, 0))
    return [
        pl.BlockSpec((1, 1, D_MODEL, SSD_COLS), lambda *a: (l, g1(*a), 0, 0)),
        pl.BlockSpec((1, D_MODEL, LANES), lambda *a: (l, 0, 0)),
        pl.BlockSpec((1, 1, LANES), lambda *a: (l, 0, 0)),
        pl.BlockSpec((1, 1, LANES), lambda *a: (l, 0, 0)),
        whole(SSD_CONV, XBC_G), whole(1, XBC_G), whole(1, GROUP_W), whole(1, GROUP_W),
        pl.BlockSpec((1, 1, GROUP_W, D_MODEL), lambda *a: (l, g2(*a), 0, 0)),
    ]


def _ssd_weight_args(w):
    return [w["w_ssd"], w["w_dt"], w["dtb"], w["alog"], w["ssd_cw"], w["ssd_cb"], w["dvec"], w["ssd_nw"], w["wo"]]


def _ssd_scratch(tm, slots):
    return ([pltpu.VMEM((slots, tm, GROUP_W), F32), pltpu.VMEM((slots, HALO + tm, XBC_G), F32)]
            + [pltpu.VMEM((tm, LANES), F32)] * 4 + [pltpu.VMEM((LANES, tm), F32)] * 2)


def _ssd_prompt_call(l, h, xg, w, *, batch, tm):
    m = h.shape[0]
    nt = m // (batch * tm)
    row = lambda b, t, s: (b * nt + t, 0)
    g1 = lambda b, t, s: jnp.minimum(s, SSD_GROUPS - 1)
    g2 = lambda b, t, s: jnp.maximum(s - 1, 0)
    return pl.pallas_call(
        _ssd_prompt_kernel,
        out_shape=(jax.ShapeDtypeStruct((m, D_MODEL), F32),
                   jax.ShapeDtypeStruct((batch * nt, SSD_GROUPS, HALO, XBC_G), F32),
                   jax.ShapeDtypeStruct((batch * nt, SSD_HEADS, SSD_HEAD_DIM, D_STATE), F32)),
        grid=(batch, nt, SSD_GROUPS + 1),
        in_specs=[pl.BlockSpec((tm, D_MODEL), row), pl.BlockSpec((tm, D_MODEL), row)]
        + _ssd_weight_specs(l, g1, g2),
        out_specs=(
            pl.BlockSpec((tm, D_MODEL), row),
            pl.BlockSpec((1, 1, HALO, XBC_G), lambda b, t, s: (b * nt + t, g2(b, t, s), 0, 0)),
            pl.BlockSpec((1, HEADS_PER_GROUP, SSD_HEAD_DIM, D_STATE),
                         lambda b, t, s: (b * nt + t, g2(b, t, s), 0, 0)),
        ),
        scratch_shapes=_ssd_scratch(tm, 2) + [pltpu.VMEM((SSD_GROUPS, HALO, XBC_G), F32),
                                              pltpu.VMEM((SSD_GROUPS, GROUP_W, D_STATE), F32)],
        compiler_params=pltpu.CompilerParams(
            dimension_semantics=("arbitrary", "arbitrary", "arbitrary"), vmem_limit_bytes=VMEM_LIMIT),
        name="ssd_prompt_group",
    )(h, xg, *_ssd_weight_args(w))


def _ssd_sample_call(l, h, xg, w, halo, h0, *, n_seq, lt):
    m = h.shape[0]
    tm = n_seq * lt
    nt = m // tm
    n_all = m // lt
    kern = functools.partial(_ssd_sample_kernel, n_seq=n_seq)
    row = lambda i, g: (i, 0)
    gi = lambda i, g: g
    return pl.pallas_call(
        kern,
        out_shape=(jax.ShapeDtypeStruct((m, D_MODEL), F32),
                   jax.ShapeDtypeStruct((SSD_GROUPS, n_all, HALO, XBC_G), F32),
                   jax.ShapeDtypeStruct((n_all, SSD_HEADS, SSD_HEAD_DIM, D_STATE), F32)),
        grid=(nt, SSD_GROUPS),
        in_specs=[pl.BlockSpec((tm, D_MODEL), row), pl.BlockSpec((tm, D_MODEL), row)]
        + _ssd_weight_specs(l, gi, gi)
        + [pl.BlockSpec((1, n_seq, HALO, XBC_G), lambda i, g: (g, i, 0, 0)),
           pl.BlockSpec((1, n_seq, HEADS_PER_GROUP, SSD_HEAD_DIM, D_STATE), lambda i, g: (l, i, g, 0, 0))],
        out_specs=(
            pl.BlockSpec((tm, D_MODEL), row),
            pl.BlockSpec((1, n_seq, HALO, XBC_G), lambda i, g: (g, i, 0, 0)),
            pl.BlockSpec((n_seq, HEADS_PER_GROUP, SSD_HEAD_DIM, D_STATE), lambda i, g: (i, g, 0, 0)),
        ),
        scratch_shapes=_ssd_scratch(tm, 1),
        compiler_params=pltpu.CompilerParams(
            dimension_semantics=("arbitrary", "arbitrary"), vmem_limit_bytes=VMEM_LIMIT),
        name="ssd_sample_group",
    )(h, xg, *_ssd_weight_args(w), halo, h0)


def _ffn_kernel(*refs, n_seq, carried, final):
    if carried:
        (x_ref, n2_ref, wu_ref, cw_ref, cb_ref, wd_ref, fn_ref,
         xo_ref, lg_ref, lv_ref, h_scr, ug_scr, uv_scr, carry_g, carry_v) = refs
    else:
        (x_ref, n2_ref, wu_ref, cw_ref, cb_ref, wd_ref, fn_ref, hg_ref, hv_ref,
         xo_ref, lg_ref, lv_ref, h_scr, ug_scr, uv_scr) = refs
    nax = 3 if carried else 2
    s = pl.program_id(nax - 1)
    tm = x_ref.shape[0]
    lt = tm // n_seq
    tf = TF

    def first():
        x = x_ref[...]
        h_scr[...] = _rms(x, n2_ref[0]).astype(BF16)
        xo_ref[...] = x

    def project(slot):
        def up_g():
            ug_scr[slot] = _dot(h_scr[...], wu_ref[0, 0, :, 0:tf])

        def up_v():
            uv_scr[slot] = _dot(h_scr[...], wu_ref[0, 0, :, tf:2 * tf])

        return [up_g, up_v]

    def conv(u, halo, cw, cb):
        if n_seq == 1:
            xp = jnp.concatenate([halo[0], u], axis=0)
            acc = cb + xp[HALO - 2:HALO - 2 + tm] * cw[0:1]
            acc = acc + xp[HALO - 1:HALO - 1 + tm] * cw[1:2]
            acc = acc + xp[HALO:] * cw[2:3]
            return acc, u[tm - HALO:].reshape(1, HALO, tf)
        u3 = u.reshape(n_seq, lt, tf)
        xp = jnp.concatenate([halo, u3], axis=1)
        acc = cb + xp[:, HALO - 2:HALO - 2 + lt] * cw[0:1]
        acc = acc + xp[:, HALO - 1:HALO - 1 + lt] * cw[1:2]
        acc = acc + xp[:, HALO:] * cw[2:3]
        return acc.reshape(tm, tf), u3[:, lt - HALO:]

    def gate(slot, f, fills):
        fills = list(fills)
        if fills:
            fills.pop(0)()
        if carried:
            halo_g = carry_g[f].reshape(1, HALO, tf)
            halo_v = carry_v[f].reshape(1, HALO, tf)
        else:
            halo_g = hg_ref[...]
            halo_v = hv_ref[...]
        cw = cw_ref[0, f]
        cb = cb_ref[0, f]
        cg, last_g = conv(ug_scr[slot], halo_g, cw[:, :tf], cb[:, :tf])
        if fills:
            fills.pop(0)()
        cv, last_v = conv(uv_scr[slot], halo_v, cw[:, tf:], cb[:, tf:])
        lg_ref[...] = last_g
        lv_ref[...] = last_v
        if carried:
            carry_g[f] = last_g[0]
            carry_v[f] = last_v[0]
        act = (_silu(cg) * cv).astype(BF16)
        xo_ref[...] += _dot(act, wd_ref[0])

    if carried:
        @pl.when((s == 0) & (pl.program_id(1) == 0))
        def _():
            carry_g[...] = jnp.zeros(carry_g.shape, F32)
            carry_v[...] = jnp.zeros(carry_v.shape, F32)

    _run_skewed(s, NF, first, project, gate)

    if final:
        @pl.when(s == NF)
        def _():
            xo_ref[...] = _rms(xo_ref[...], fn_ref[...])


def _ffn_call(l, x, w, fnw, halos, *, batch, tm, n_seq, final):
    m = x.shape[0]
    carried = halos is None
    kern = functools.partial(_ffn_kernel, n_seq=n_seq, carried=carried, final=final)
    if carried:
        nt = m // (batch * tm)
        grid = (batch, nt, NF + 1)
        row = lambda b, t, s: (b * nt + t, 0)
        step = lambda b, t, s: s
        si = lambda b, t, s: b * nt + t
        n_state = batch * nt
    else:
        grid = (m // tm, NF + 1)
        row = lambda i, s: (i, 0)
        step = lambda i, s: s
        si = lambda i, s: i
        n_state = m // (tm // n_seq)
    f1 = lambda *a: jnp.minimum(step(*a), NF - 1)
    fi = lambda *a: jnp.maximum(step(*a) - 1, 0)
    in_specs = [
        pl.BlockSpec((tm, D_MODEL), row),
        pl.BlockSpec((1, 1, D_MODEL), lambda *a: (l, 0, 0)),
        pl.BlockSpec((1, 1, D_MODEL, 2 * TF), lambda *a: (l, f1(*a), 0, 0)),
        pl.BlockSpec((1, NF, FFN_CONV, 2 * TF), lambda *a: (l, 0, 0, 0)),
        pl.BlockSpec((1, NF, 1, 2 * TF), lambda *a: (l, 0, 0, 0)),
        pl.BlockSpec((1, TF, D_MODEL), lambda *a: (l, fi(*a), 0)),
        pl.BlockSpec((1, D_MODEL), lambda *a: (0, 0)),
    ]
    args = [x, w["n2"], w["w_up"], w["ffn_cw"], w["ffn_cb"], w["w_down"], fnw]
    scratch = [pltpu.VMEM((tm, D_MODEL), BF16), pltpu.VMEM((2, tm, TF), F32), pltpu.VMEM((2, tm, TF), F32)]
    if carried:
        scratch += [pltpu.VMEM((NF, HALO, TF), F32), pltpu.VMEM((NF, HALO, TF), F32)]
    else:
        in_specs += [pl.BlockSpec((n_seq, HALO, TF), lambda *a: (si(*a), 0, fi(*a))),
                     pl.BlockSpec((n_seq, HALO, TF), lambda *a: (si(*a), 0, fi(*a) + NF))]
        args += [halos, halos]
    st_rows = 1 if carried else n_seq
    return pl.pallas_call(
        kern,
        out_shape=(jax.ShapeDtypeStruct((m, D_MODEL), F32),
                   jax.ShapeDtypeStruct((n_state, HALO, D_FF), F32),
                   jax.ShapeDtypeStruct((n_state, HALO, D_FF), F32)),
        grid=grid,
        in_specs=in_specs,
        out_specs=(
            pl.BlockSpec((tm, D_MODEL), row),
            pl.BlockSpec((st_rows, HALO, TF), lambda *a: (si(*a), 0, fi(*a))),
            pl.BlockSpec((st_rows, HALO, TF), lambda *a: (si(*a), 0, fi(*a))),
        ),
        scratch_shapes=scratch,
        compiler_params=pltpu.CompilerParams(
            dimension_semantics=("arbitrary",) * len(grid), vmem_limit_bytes=VMEM_LIMIT),
        name="ffn_prompt" if carried else "ffn_sample",
    )(*args)


def _group_cols(w, n_groups):
    return w.reshape(w.shape[:-1] + (n_groups, w.shape[-1] // n_groups))


def _xbc_to_groups(a):
    xs = _group_cols(a[..., :D_SSD], SSD_GROUPS)
    bs = _group_cols(a[..., D_SSD:D_SSD + SSD_GROUPS * D_STATE], SSD_GROUPS)
    cs = _group_cols(a[..., D_SSD + SSD_GROUPS * D_STATE:], SSD_GROUPS)
    return jnp.concatenate([xs, bs, cs], axis=-1)


def _groups_to_xbc(a):
    lead = a.shape[:-2]
    xs = a[..., :GROUP_W].reshape(lead + (D_SSD,))
    bs = a[..., GROUP_W:GROUP_W + D_STATE].reshape(lead + (SSD_GROUPS * D_STATE,))
    cs = a[..., GROUP_W + D_STATE:].reshape(lead + (SSD_GROUPS * D_STATE,))
    return jnp.concatenate([xs, bs, cs], axis=-1)


def _ffn_blocks(a):
    return jnp.concatenate([_group_cols(a[..., :D_FF], NF), _group_cols(a[..., D_FF:], NF)], axis=-1)


def _pad_halo(state):
    return jnp.pad(state, ((0, 0), (HALO - state.shape[1], 0), (0, 0)))


def _weights(norm1_w, w_in, ssd_conv_w, ssd_conv_b, dt_bias, a_log, ssd_d, ssd_norm_w, gmlp_norm_w, w_out, norm2_w,
             w_up, ffn_conv_w, ffn_conv_b, w_down):
    d = w_in.shape[0]
    lanes_pad = ((0, 0), (0, 0), (0, LANES - SSD_HEADS))
    w_ssd = jnp.concatenate([
        _group_cols(w_in[:, :, :S_X], SSD_GROUPS), _group_cols(w_in[:, :, S_X:S_B], SSD_GROUPS),
        _group_cols(w_in[:, :, S_B:S_C], SSD_GROUPS), _group_cols(w_in[:, :, S_C:S_DT], SSD_GROUPS)], axis=-1)
    w_gm = jnp.concatenate([_group_cols(w_in[:, :, S_U:S_V], GMLP_GROUPS),
                            _group_cols(w_in[:, :, S_V:], GMLP_GROUPS)], axis=-1)
    return dict(
        n1=norm1_w[:, None, :], n2=norm2_w[:, None, :],
        w_ssd=w_ssd.transpose(0, 2, 1, 3).astype(BF16),
        w_dt=jnp.pad(w_in[:, :, S_DT:S_U], lanes_pad).astype(BF16),
        w_gm=w_gm.transpose(0, 2, 1, 3).astype(BF16),
        wo=w_out.astype(BF16).reshape(d, 2 * SSD_GROUPS, GROUP_W, D_MODEL),
        w_up=_ffn_blocks(w_up).transpose(0, 2, 1, 3).astype(BF16),
        w_down=w_down.astype(BF16),
        dtb=jnp.pad(dt_bias, ((0, 0), (0, LANES - SSD_HEADS)))[:, None, :],
        alog=jnp.pad(a_log, ((0, 0), (0, LANES - SSD_HEADS)))[:, None, :],
        ssd_cw=_xbc_to_groups(ssd_conv_w).transpose(0, 2, 1, 3),
        ssd_cb=_xbc_to_groups(ssd_conv_b)[:, :, None, :],
        dvec=jnp.repeat(ssd_d, SSD_HEAD_DIM, axis=-1).reshape(d, SSD_GROUPS, 1, GROUP_W),
        ssd_nw=ssd_norm_w.reshape(d, SSD_GROUPS, 1, GROUP_W),
        gnw=gmlp_norm_w.reshape(d, GMLP_GROUPS, 1, GROUP_W),
        ffn_cw=_ffn_blocks(ffn_conv_w).transpose(0, 2, 1, 3),
        ffn_cb=_ffn_blocks(ffn_conv_b)[:, :, None, :],
    )


def kernel(x_prompt, x_sample, state_ssd_conv, state_ssd, state_ffn_conv, norm1_w, w_in, ssd_conv_w, ssd_conv_b,
           dt_bias, a_log, ssd_d, ssd_norm_w, gmlp_norm_w, gmlp_w_s, gmlp_b_s, w_out, norm2_w, w_up, ffn_conv_w,
           ffn_conv_b, w_down, final_norm_w):
    bp, lp, _ = x_prompt.shape
    bs, ls, _ = x_sample.shape
    xp = x_prompt.reshape(bp * lp, D_MODEL)
    xs = x_sample.reshape(bs * ls, D_MODEL)
    ms = bs * ls
    fnw = final_norm_w[None, :]
    ssd_seqs = 16
    ntp = lp // TM
    w = _weights(norm1_w, w_in, ssd_conv_w, ssd_conv_b, dt_bias, a_log, ssd_d, ssd_norm_w, gmlp_norm_w, w_out,
                 norm2_w, w_up, ffn_conv_w, ffn_conv_b, w_down)
    ws_p = gmlp_w_s
    bs_p = gmlp_b_s[:, :, :, None]
    ws_s = jnp.tile(gmlp_w_s[:, :, :ls, :ls], (1, 1, 1, bs))
    bs_s = jnp.tile(gmlp_b_s[:, :, :ls], (1, 1, bs))[:, :, :, None]
    p_conv, p_ssd, p_ffn = [], [], []
    s_conv, s_ssd, s_ffn, s_v = [], [], [], []
    for l in range(DEPTH):
        final = l == DEPTH - 1
        xg, h = _gmlp_call(l, xp, w, ws_p, bs_p, tm=TM, chunk=GMLP_CHUNK, seg_len=GMLP_CHUNK, emit_vn=False)
        x1, cst, hfin = _ssd_prompt_call(l, h, xg, w, batch=bp, tm=TM)
        xp, lg, lv = _ffn_call(l, x1, w, fnw, None, batch=bp, tm=TM, n_seq=1, final=final)
        cst, hfin, lg, lv = (a.reshape((bp, ntp) + a.shape[1:])[:, -1] for a in (cst, hfin, lg, lv))
        p_conv.append(_groups_to_xbc(cst[:, :, HALO - (SSD_CONV - 1):, :].transpose(0, 2, 1, 3)))
        p_ssd.append(hfin)
        p_ffn.append(jnp.concatenate([lg, lv], axis=-1)[:, HALO - (FFN_CONV - 1):, :])
        xg, h, vn = _gmlp_call(l, xs, w, ws_s, bs_s, tm=ms, chunk=ms, seg_len=ls, emit_vn=True)
        halo = _pad_halo(_xbc_to_groups(state_ssd_conv[l]).transpose(2, 0, 1, 3).reshape(
            SSD_GROUPS * bs, SSD_CONV - 1, XBC_G)).reshape(SSD_GROUPS, bs, HALO, XBC_G)
        x1, cst, hfin = _ssd_sample_call(l, h, xg, w, halo, state_ssd, n_seq=ssd_seqs, lt=ls)
        xs, lg, lv = _ffn_call(l, x1, w, fnw, _pad_halo(state_ffn_conv[l]), batch=1, tm=ms, n_seq=bs, final=final)
        s_conv.append(_groups_to_xbc(cst[:, :, HALO - (SSD_CONV - 1):, :].transpose(1, 2, 0, 3)))
        s_ssd.append(hfin)
        s_ffn.append(jnp.concatenate([lg, lv], axis=-1)[:, HALO - (FFN_CONV - 1):, :])
        s_v.append(vn.reshape(bs, ls, D_GMLP))
    return (xp.reshape(bp, lp, D_MODEL), xs.reshape(bs, ls, D_MODEL),
            jnp.stack(p_conv), jnp.stack(p_ssd), jnp.stack(p_ffn),
            jnp.stack(s_conv), jnp.stack(s_ssd), jnp.stack(s_ffn), jnp.stack(s_v))
```

```python
import functools

import jax
import jax.numpy as jnp
from jax import lax
from jax.experimental import pallas as pl
from jax.experimental.pallas import tpu as pltpu

D_MODEL = 2048
DEPTH = 2
CHUNK = 64
D_SSD = 2048
SSD_HEAD_DIM = 64
SSD_HEADS = 32
SSD_GROUPS = 8
HEADS_PER_GROUP = 4
D_STATE = 128
SSD_CONV = 4
D_XBC = 4096
D_GMLP = 2048
GMLP_GROUPS = 8
GMLP_CHUNK = 128
D_FF = 5632
FFN_CONV = 3
EPS = 1e-6

LANES = 128
GROUP_W = 256
XBC_G = GROUP_W + 2 * D_STATE
SSD_COLS = GROUP_W + XBC_G
HALO = 8
TM = 512
TM_WIDE = 1024
SSD_Q = 256
TF = 512
NF = D_FF // TF
VMEM_LIMIT = 60 * 1024 * 1024

S_X = D_SSD
S_B = S_X + D_SSD
S_C = S_B + SSD_GROUPS * D_STATE
S_DT = S_C + SSD_GROUPS * D_STATE
S_U = S_DT + SSD_HEADS
S_V = S_U + D_GMLP

F32 = jnp.float32
BF16 = jnp.bfloat16


def _dot(a, b):
    return jnp.dot(a, b, preferred_element_type=F32)


def _dot_nt(a, b):
    return lax.dot_general(a, b, (((1,), (1,)), ((), ())), preferred_element_type=F32)


def _dot_tn(a, b):
    return lax.dot_general(a, b, (((0,), (0,)), ((), ())), preferred_element_type=F32)


def _rms(x, w):
    return x * lax.rsqrt(jnp.mean(x * x, axis=-1, keepdims=True) + EPS) * w


def _silu(x):
    return x * jax.nn.sigmoid(x)


def _softplus(x):
    return jnp.maximum(x, 0.0) + jnp.log1p(jnp.exp(-jnp.abs(x)))


def _seg_cumsum(x, seg_len):
    pos = lax.broadcasted_iota(jnp.int32, x.shape, 0) & (seg_len - 1)
    k = 1
    while k < seg_len:
        x = x + jnp.where(pos >= k, pltpu.roll(x, k, 0), 0.0)
        k *= 2
    return x


def _seg_total(x, seg_len):
    n = x.shape[0]
    pos = lax.broadcasted_iota(jnp.int32, x.shape, 0) & (seg_len - 1)
    k = 1
    while k < seg_len:
        x = x + jnp.where((pos & k) == 0, pltpu.roll(x, n - k, 0), pltpu.roll(x, k, 0))
        k *= 2
    return x


def _head_cols(v, width):
    n = v.shape[0]
    lane_head = lax.broadcasted_iota(jnp.int32, (n, width), 1) >> 6
    out = jnp.broadcast_to(v[:, 3:4], (n, width))
    for e in (2, 1, 0):
        out = jnp.where(lane_head == e, jnp.broadcast_to(v[:, e:e + 1], (n, width)), out)
    return out


def _head_rows(row):
    row_head = lax.broadcasted_iota(jnp.int32, (GROUP_W, D_STATE), 0) >> 6
    out = jnp.broadcast_to(row[:, 3:4], (GROUP_W, D_STATE))
    for e in (2, 1, 0):
        out = jnp.where(row_head == e, jnp.broadcast_to(row[:, e:e + 1], (GROUP_W, D_STATE)), out)
    return out


def _run_skewed(s, n, first, stage1, stage2):
    @pl.when(s == 0)
    def _():
        first()
        for piece in stage1(0):
            piece()

    @pl.when((s > 0) & (s < n))
    def _():
        stage2((s - 1) & 1, s - 1, stage1(s & 1))

    @pl.when(s == n)
    def _():
        stage2((n - 1) & 1, n - 1, [])


def _gmlp_kernel(x_ref, n1_ref, wu_ref, wv_ref, gnw_ref, ws_ref, bs_ref, wo_ref, xo_ref, h_ref, *rest,
                 chunk, seg_len, emit_vn):
    vn_ref = rest[0] if emit_vn else None
    h_scr = rest[-1]
    g = pl.program_id(1)
    tm = x_ref.shape[0]
    seg_shift = seg_len.bit_length() - 1
    chunk_shift = CHUNK.bit_length() - 1

    @pl.when(g == 0)
    def _():
        x = x_ref[...]
        hb = _rms(x, n1_ref[0]).astype(BF16)
        h_scr[...] = hb
        h_ref[...] = hb
        xo_ref[...] = x

    hb = h_scr[...]
    u = jax.nn.gelu(_dot(hb, wu_ref[0]))
    v = jax.nn.gelu(_dot(hb, wv_ref[0]))
    vn = _rms(v, gnw_ref[0, g])
    if emit_vn:
        vn_ref[...] = vn
    ii = lax.broadcasted_iota(jnp.int32, (chunk, chunk), 0)
    jj = lax.broadcasted_iota(jnp.int32, (chunk, chunk), 1)
    same_seq = (ii >> seg_shift) == (jj >> seg_shift)
    block_causal = ((jj & (seg_len - 1)) >> chunk_shift) <= ((ii & (seg_len - 1)) >> chunk_shift)
    w_rows = ws_ref[0, 0]
    w_full = w_rows if seg_len == chunk else jnp.concatenate([w_rows] * (chunk // seg_len), axis=0)
    wm = jnp.where(same_seq & block_causal, w_full, 0.0).astype(BF16)
    vnb = vn.astype(BF16)
    bias = bs_ref[0, g]
    parts = []
    for c in range(tm // chunk):
        sg = _dot(wm, vnb[c * chunk:(c + 1) * chunk]) + bias
        parts.append(u[c * chunk:(c + 1) * chunk] * sg)
    y = parts[0] if len(parts) == 1 else jnp.concatenate(parts, axis=0)
    xo_ref[...] += _dot(y.astype(BF16), wo_ref[0, 0])


def _x_tile_spec(tm, index_map, single):
    if single:
        return pl.BlockSpec((tm, D_MODEL), index_map, pipeline_mode=pl.Buffered(1))
    return pl.BlockSpec((tm, D_MODEL), index_map)


def _gmlp_call(l, x, w, ws, bs, *, tm, chunk, seg_len, emit_vn):
    m = x.shape[0]
    nt = m // tm
    kern = functools.partial(_gmlp_kernel, chunk=chunk, seg_len=seg_len, emit_vn=emit_vn)
    out_shape = [jax.ShapeDtypeStruct((m, D_MODEL), F32), jax.ShapeDtypeStruct((m, D_MODEL), BF16)]
    out_specs = [pl.BlockSpec((tm, D_MODEL), lambda i, g: (i, 0)), pl.BlockSpec((tm, D_MODEL), lambda i, g: (i, 0))]
    if emit_vn:
        out_shape.append(jax.ShapeDtypeStruct((m, D_GMLP), F32))
        out_specs.append(pl.BlockSpec((tm, GROUP_W), lambda i, g: (i, g)))
    return pl.pallas_call(
        kern,
        out_shape=tuple(out_shape),
        grid=(nt, GMLP_GROUPS),
        in_specs=[
            _x_tile_spec(tm, lambda i, g: (i, 0), tm > TM),
            pl.BlockSpec((1, 1, D_MODEL), lambda i, g: (l, 0, 0)),
            pl.BlockSpec((1, D_MODEL, GROUP_W), lambda i, g: (l, 0, g)),
            pl.BlockSpec((1, D_MODEL, GROUP_W), lambda i, g: (l, 0, GMLP_GROUPS + g)),
            pl.BlockSpec((1, GMLP_GROUPS, 1, GROUP_W), lambda i, g: (l, 0, 0, 0)),
            pl.BlockSpec((1, 1, seg_len, chunk), lambda i, g: (l, g, 0, 0)),
            pl.BlockSpec((1, GMLP_GROUPS, chunk, 1), lambda i, g: (l, 0, 0, 0)),
            pl.BlockSpec((1, 1, GROUP_W, D_MODEL), lambda i, g: (l, SSD_GROUPS + g, 0, 0)),
        ],
        out_specs=tuple(out_specs),
        scratch_shapes=[pltpu.VMEM((tm, D_MODEL), BF16)],
        compiler_params=pltpu.CompilerParams(
            dimension_semantics=("arbitrary", "arbitrary"), vmem_limit_bytes=VMEM_LIMIT),
        name="gmlp_group",
    )(x, w["n1"], w["w_uv"], w["w_uv"], w["gnw"], ws, bs, w["wo"])


def _ssd_heads(h_ref, wdt_ref, dtb_ref, alog_ref, seg_len, hd):
    acs_scr, eacs_scr, dend_scr, etot_scr, acs_t_scr, dt_t_scr = hd
    dt = _softplus(_dot(h_ref[...], wdt_ref[0]) + dtb_ref[0])
    dta = dt * (-jnp.exp(alog_ref[0]))
    acs = _seg_cumsum(dta, seg_len)
    tot = _seg_total(dta, seg_len)
    acs_scr[...] = acs
    eacs_scr[...] = jnp.exp(acs)
    dend_scr[...] = jnp.exp(tot - acs) * dt
    etot_scr[...] = jnp.exp(tot)
    acs_t_scr[...] = acs.T
    dt_t_scr[...] = dt.T


def _group_lanes(v, g):
    return pltpu.roll(v, (LANES - HEADS_PER_GROUP * g) & (LANES - 1), 1)


def _ssd_stage1(h_ref, w_refs, pz_scr, pxbc_scr, slot):
    wz_ref, wx_ref, wb_ref, wc_ref = w_refs
    tm = h_ref.shape[0]
    rows = pl.ds(HALO, tm)

    def proj_z():
        pz_scr[slot] = _dot(h_ref[...], wz_ref[0])

    def proj_x():
        pxbc_scr[slot, rows, 0:GROUP_W] = _dot(h_ref[...], wx_ref[0])

    def proj_bc():
        w_bc = jnp.concatenate([wb_ref[0], wc_ref[0]], axis=1)
        pxbc_scr[slot, rows, GROUP_W:XBC_G] = _dot(h_ref[...], w_bc)

    return [proj_z, proj_x, proj_bc]


def _ssd_finish(g, y, xs, z, dvec_ref, nw_ref, wo_ref, xo_ref):
    y = y + dvec_ref[0, g] * xs
    y = y * _silu(z)
    y = _rms(y, nw_ref[0, g])
    xo_ref[...] += _dot(y.astype(BF16), wo_ref[0, 0])


def _intra(cb, acs, e, acs_row, dt_row, mask):
    seg = acs[:, e:e + 1] - acs_row
    if mask is not None:
        seg = jnp.where(mask, seg, -jnp.inf)
    return (cb * jnp.exp(seg) * dt_row).astype(BF16)


def _head_row(t_scr, g, e, c0, n):
    return t_scr[pl.ds(HEADS_PER_GROUP * g + e, 1), c0:c0 + n]


def _ssd_stage2_prompt(slot, g, tm, pz_scr, pxbc_scr, hd, carry, state, cw_ref, cb_ref,
                       dvec_ref, nw_ref, wo_ref, xo_ref, cst_ref, hfin_ref, fills):
    q = SSD_Q
    half = q // 2
    fills = list(fills)
    acs_scr, eacs_scr, dend_scr, etot_scr, acs_t_scr, dt_t_scr = hd
    if fills:
        fills.pop(0)()
    cw = cw_ref[0, g]
    pxbc_scr[slot, 0:HALO, :] = carry[g]
    acc = cb_ref[0, g] + pxbc_scr[slot, pl.ds(HALO - 3, tm), :] * cw[0:1]
    for k in (1, 2, 3):
        acc = acc + pxbc_scr[slot, pl.ds(HALO - 3 + k, tm), :] * cw[k:k + 1]
    last = pxbc_scr[slot, pl.ds(tm, HALO), :]
    carry[g] = last
    cst_ref[0, 0] = last
    xbc = _silu(acc)
    xs = xbc[:, :GROUP_W]
    xsb = xs.astype(BF16)
    bb = xbc[:, GROUP_W:GROUP_W + D_STATE].astype(BF16)
    cc = xbc[:, GROUP_W + D_STATE:].astype(BF16)
    ii = lax.broadcasted_iota(jnp.int32, (half, half), 0)
    jj = lax.broadcasted_iota(jnp.int32, (half, half), 1)
    tri = ii >= jj
    lane_head = lax.broadcasted_iota(jnp.int32, (q, GROUP_W), 1) >> 6
    hst = state[g]
    ys = []
    for c in range(tm // q):
        if fills:
            fills.pop(0)()
        r0 = c * q
        sl = slice(r0, r0 + q)
        acs = _group_lanes(acs_scr[sl], g)
        scale = _head_cols(_group_lanes(eacs_scr[sl], g), GROUP_W)
        dend = _head_cols(_group_lanes(dend_scr[sl], g), GROUP_W)
        etot = _group_lanes(etot_scr[r0:r0 + HALO], g)[0:1]
        cbm = _dot_nt(cc[sl], bb[sl])
        top, bot = None, None
        for e in range(HEADS_PER_GROUP):
            arow = _head_row(acs_t_scr, g, e, r0, q)
            drow = _head_row(dt_t_scr, g, e, r0, q)
            xe = jnp.where(lane_head == e, xsb[sl], jnp.zeros((q, GROUP_W), BF16))
            m00 = _intra(cbm[:half, :half], acs[:half], e, arow[:, :half], drow[:, :half], tri)
            m10 = _intra(cbm[half:, :half], acs[half:], e, arow[:, :half], drow[:, :half], None)
            m11 = _intra(cbm[half:, half:], acs[half:], e, arow[:, half:], drow[:, half:], tri)
            d0 = _dot(m00, xe[:half])
            d1 = _dot(jnp.concatenate([m10, m11], axis=1), xe)
            top = d0 if top is None else top + d0
            bot = d1 if bot is None else bot + d1
        ydiag = jnp.concatenate([top, bot], axis=0)
        yoff = _dot_nt(cc[sl], hst.astype(BF16)) * scale
        ys.append(ydiag + yoff)
        s_new = _dot_tn((xs[sl] * dend).astype(BF16), bb[sl])
        hst = hst * _head_rows(etot) + s_new
    for fill in fills:
        fill()
    state[g] = hst
    hfin_ref[0] = hst.reshape(HEADS_PER_GROUP, SSD_HEAD_DIM, D_STATE)
    y = ys[0] if len(ys) == 1 else jnp.concatenate(ys, axis=0)
    _ssd_finish(g, y, xs, pz_scr[slot], dvec_ref, nw_ref, wo_ref, xo_ref)


def _ssd_prompt_kernel(h_ref, xg_ref, wz_ref, wx_ref, wb_ref, wc_ref, wdt_ref, dtb_ref, alog_ref, cw_ref, cb_ref, dvec_ref, nw_ref, wo_ref,
                       xo_ref, cst_ref, hfin_ref, pz_scr, pxbc_scr, *scr):
    hd, (carry, state) = scr[:6], scr[6:]
    t = pl.program_id(1)
    s = pl.program_id(2)
    tm = h_ref.shape[0]

    @pl.when((s == 0) & (t == 0))
    def _():
        carry[...] = jnp.zeros(carry.shape, F32)
        state[...] = jnp.zeros(state.shape, F32)

    def first():
        xo_ref[...] = xg_ref[...]
        _ssd_heads(h_ref, wdt_ref, dtb_ref, alog_ref, SSD_Q, hd)

    def stage1(slot):
        return _ssd_stage1(h_ref, (wz_ref, wx_ref, wb_ref, wc_ref), pz_scr, pxbc_scr, slot)

    def stage2(slot, g, fills):
        _ssd_stage2_prompt(slot, g, tm, pz_scr, pxbc_scr, hd, carry, state, cw_ref, cb_ref,
                           dvec_ref, nw_ref, wo_ref, xo_ref, cst_ref, hfin_ref, fills)

    _run_skewed(s, SSD_GROUPS, first, stage1, stage2)


def _ssd_sample_kernel(h_ref, xg_ref, wz_ref, wx_ref, wb_ref, wc_ref, wdt_ref, dtb_ref, alog_ref, cw_ref, cb_ref, dvec_ref, nw_ref, wo_ref,
                       halo_ref, h0_ref, xo_ref, cst_ref, hfin_ref, pz_scr, pxbc_scr, *hd, n_seq):
    acs_scr, eacs_scr, dend_scr, etot_scr, acs_t_scr, dt_t_scr = hd
    g = pl.program_id(1)
    tm = h_ref.shape[0]
    lt = tm // n_seq
    lt_shift = lt.bit_length() - 1

    @pl.when(g == 0)
    def _():
        xo_ref[...] = xg_ref[...]
        _ssd_heads(h_ref, wdt_ref, dtb_ref, alog_ref, lt, hd)

    for piece in _ssd_stage1(h_ref, (wz_ref, wx_ref, wb_ref, wc_ref), pz_scr, pxbc_scr, 0):
        piece()
    cw = cw_ref[0, g]
    pre3 = pxbc_scr[0, pl.ds(HALO, tm), :].reshape(n_seq, lt, XBC_G)
    xp = jnp.concatenate([halo_ref[0], pre3], axis=1)
    acc = cb_ref[0, g] + xp[:, HALO - 3:HALO - 3 + lt] * cw[0:1]
    for k in (1, 2, 3):
        acc = acc + xp[:, HALO - 3 + k:HALO - 3 + k + lt] * cw[k:k + 1]
    cst_ref[0] = pre3[:, lt - HALO:]
    xbc = _silu(acc.reshape(tm, XBC_G))
    xs = xbc[:, :GROUP_W]
    xsb = xs.astype(BF16)
    bb = xbc[:, GROUP_W:GROUP_W + D_STATE].astype(BF16)
    cc = xbc[:, GROUP_W + D_STATE:].astype(BF16)
    ii = lax.broadcasted_iota(jnp.int32, (tm, tm), 0)
    jj = lax.broadcasted_iota(jnp.int32, (tm, tm), 1)
    mask = ((ii >> lt_shift) == (jj >> lt_shift)) & (ii >= jj)
    acs = _group_lanes(acs_scr[...], g)
    scale = _head_cols(_group_lanes(eacs_scr[...], g), GROUP_W)
    dend = _head_cols(_group_lanes(dend_scr[...], g), GROUP_W)
    cd = _group_lanes(etot_scr[...], g)
    cbm = _dot_nt(cc, bb)
    lane_head = lax.broadcasted_iota(jnp.int32, (tm, GROUP_W), 1) >> 6
    ydiag = None
    for e in range(HEADS_PER_GROUP):
        mp = _intra(cbm, acs, e, _head_row(acs_t_scr, g, e, 0, tm), _head_row(dt_t_scr, g, e, 0, tm), mask)
        d = _dot(mp, jnp.where(lane_head == e, xsb, jnp.zeros((tm, GROUP_W), BF16)))
        ydiag = d if ydiag is None else ydiag + d
    xdtd = (xs * dend).astype(BF16)
    row_seq = lax.broadcasted_iota(jnp.int32, (tm, D_STATE), 0) >> lt_shift
    zero = jnp.zeros((tm, D_STATE), BF16)
    h0 = [h0_ref[0, i].reshape(GROUP_W, D_STATE) for i in range(n_seq)]
    h0_wide = jnp.concatenate(h0, axis=1).astype(BF16)
    c_wide = jnp.concatenate([jnp.where(row_seq == i, cc, zero) for i in range(n_seq)], axis=1)
    b_wide = jnp.concatenate([jnp.where(row_seq == i, bb, zero) for i in range(n_seq)], axis=1)
    yoff = _dot_nt(c_wide, h0_wide) * scale
    s_new = _dot_tn(xdtd, b_wide)
    for i in range(n_seq):
        hn = h0[i] * _head_rows(cd[i * lt:i * lt + 1, :]) + s_new[:, i * D_STATE:(i + 1) * D_STATE]
        hfin_ref[i] = hn.reshape(HEADS_PER_GROUP, SSD_HEAD_DIM, D_STATE)
    _ssd_finish(g, ydiag + yoff, xs, pz_scr[0], dvec_ref, nw_ref, wo_ref, xo_ref)


def _ssd_weight_specs(l, g1, g2):
    whole = lambda rows, width: pl.BlockSpec((1, SSD_GROUPS, rows, width), lambda *a: (l, 0, 0, 0))
    col = lambda width, start: pl.BlockSpec((1, D_MODEL, width), lambda *a: (l, 0, start // width + g1(*a)))
    return [
        col(GROUP_W, 0), col(GROUP_W, S_X), col(D_STATE, S_B), col(D_STATE, S_C),
        pl.BlockSpec((1, D_MODEL, LANES), lambda *a: (l, 0, S_DT // LANES)),
        pl.BlockSpec((1, 1, LANES), lambda *a: (l, 0, 0)),
        pl.BlockSpec((1, 1, LANES), lambda *a: (l, 0, 0)),
        whole(SSD_CONV, XBC_G), whole(1, XBC_G), whole(1, GROUP_W), whole(1, GROUP_W),
        pl.BlockSpec((1, 1, GROUP_W, D_MODEL), lambda *a: (l, g2(*a), 0, 0)),
    ]


def _ssd_weight_args(w):
    return [w["w_in"]] * 5 + [w["dtb"], w["alog"], w["ssd_cw"], w["ssd_cb"], w["dvec"], w["ssd_nw"], w["wo"]]


def _ssd_scratch(tm, slots):
    return ([pltpu.VMEM((slots, tm, GROUP_W), F32), pltpu.VMEM((slots, HALO + tm, XBC_G), F32)]
            + [pltpu.VMEM((tm, LANES), F32)] * 4 + [pltpu.VMEM((LANES, tm), F32)] * 2)


def _ssd_prompt_call(l, h, xg, w, *, batch, tm):
    m = h.shape[0]
    nt = m // (batch * tm)
    row = lambda b, t, s: (b * nt + t, 0)
    g1 = lambda b, t, s: jnp.minimum(s, SSD_GROUPS - 1)
    g2 = lambda b, t, s: jnp.maximum(s - 1, 0)
    return pl.pallas_call(
        _ssd_prompt_kernel,
        out_shape=(jax.ShapeDtypeStruct((m, D_MODEL), F32),
                   jax.ShapeDtypeStruct((batch * nt, SSD_GROUPS, HALO, XBC_G), F32),
                   jax.ShapeDtypeStruct((batch * nt, SSD_HEADS, SSD_HEAD_DIM, D_STATE), F32)),
        grid=(batch, nt, SSD_GROUPS + 1),
        in_specs=[pl.BlockSpec((tm, D_MODEL), row), pl.BlockSpec((tm, D_MODEL), row)]
        + _ssd_weight_specs(l, g1, g2),
        out_specs=(
            pl.BlockSpec((tm, D_MODEL), row),
            pl.BlockSpec((1, 1, HALO, XBC_G), lambda b, t, s: (b * nt + t, g2(b, t, s), 0, 0)),
            pl.BlockSpec((1, HEADS_PER_GROUP, SSD_HEAD_DIM, D_STATE),
                         lambda b, t, s: (b * nt + t, g2(b, t, s), 0, 0)),
        ),
        scratch_shapes=_ssd_scratch(tm, 2) + [pltpu.VMEM((SSD_GROUPS, HALO, XBC_G), F32),
                                              pltpu.VMEM((SSD_GROUPS, GROUP_W, D_STATE), F32)],
        compiler_params=pltpu.CompilerParams(
            dimension_semantics=("arbitrary", "arbitrary", "arbitrary"), vmem_limit_bytes=VMEM_LIMIT),
        name="ssd_prompt_group",
    )(h, xg, *_ssd_weight_args(w))


def _ssd_sample_call(l, h, xg, w, halo, h0, *, n_seq, lt):
    m = h.shape[0]
    tm = n_seq * lt
    nt = m // tm
    n_all = m // lt
    kern = functools.partial(_ssd_sample_kernel, n_seq=n_seq)
    row = lambda i, g: (i, 0)
    gi = lambda i, g: g
    return pl.pallas_call(
        kern,
        out_shape=(jax.ShapeDtypeStruct((m, D_MODEL), F32),
                   jax.ShapeDtypeStruct((SSD_GROUPS, n_all, HALO, XBC_G), F32),
                   jax.ShapeDtypeStruct((n_all, SSD_HEADS, SSD_HEAD_DIM, D_STATE), F32)),
        grid=(nt, SSD_GROUPS),
        in_specs=[pl.BlockSpec((tm, D_MODEL), row), pl.BlockSpec((tm, D_MODEL), row)]
        + _ssd_weight_specs(l, gi, gi)
        + [pl.BlockSpec((1, n_seq, HALO, XBC_G), lambda i, g: (g, i, 0, 0)),
           pl.BlockSpec((1, n_seq, HEADS_PER_GROUP, SSD_HEAD_DIM, D_STATE), lambda i, g: (l, i, g, 0, 0))],
        out_specs=(
            pl.BlockSpec((tm, D_MODEL), row),
            pl.BlockSpec((1, n_seq, HALO, XBC_G), lambda i, g: (g, i, 0, 0)),
            pl.BlockSpec((n_seq, HEADS_PER_GROUP, SSD_HEAD_DIM, D_STATE), lambda i, g: (i, g, 0, 0)),
        ),
        scratch_shapes=_ssd_scratch(tm, 1),
        compiler_params=pltpu.CompilerParams(
            dimension_semantics=("arbitrary", "arbitrary"), vmem_limit_bytes=VMEM_LIMIT),
        name="ssd_sample_group",
    )(h, xg, *_ssd_weight_args(w), halo, h0)


def _ffn_kernel(*refs, n_seq, carried, final):
    if carried:
        (x_ref, n2_ref, wg_ref, wv_ref, cw_ref, cb_ref, wd_ref, fn_ref,
         xo_ref, lg_ref, lv_ref, h_scr, carry_g, carry_v) = refs
    else:
        (x_ref, n2_ref, wg_ref, wv_ref, cw_ref, cb_ref, wd_ref, fn_ref, hg_ref, hv_ref,
         xo_ref, lg_ref, lv_ref, h_scr) = refs
    nax = 3 if carried else 2
    f = pl.program_id(nax - 1)
    tm = x_ref.shape[0]
    lt = tm // n_seq
    tf = TF

    @pl.when(f == 0)
    def _():
        x = x_ref[...]
        h_scr[...] = _rms(x, n2_ref[0]).astype(BF16)
        xo_ref[...] = x

    def conv(u, halo, cw, cb):
        if n_seq == 1:
            xp = jnp.concatenate([halo[0], u], axis=0)
            acc = cb + xp[HALO - 2:HALO - 2 + tm] * cw[0:1]
            acc = acc + xp[HALO - 1:HALO - 1 + tm] * cw[1:2]
            acc = acc + xp[HALO:] * cw[2:3]
            return acc, u[tm - HALO:].reshape(1, HALO, tf)
        u3 = u.reshape(n_seq, lt, tf)
        xp = jnp.concatenate([halo, u3], axis=1)
        acc = cb + xp[:, HALO - 2:HALO - 2 + lt] * cw[0:1]
        acc = acc + xp[:, HALO - 1:HALO - 1 + lt] * cw[1:2]
        acc = acc + xp[:, HALO:] * cw[2:3]
        return acc.reshape(tm, tf), u3[:, lt - HALO:]

    if carried:
        @pl.when((f == 0) & (pl.program_id(1) == 0))
        def _():
            carry_g[...] = jnp.zeros(carry_g.shape, F32)
            carry_v[...] = jnp.zeros(carry_v.shape, F32)
        halo_g = carry_g[f].reshape(1, HALO, tf)
        halo_v = carry_v[f].reshape(1, HALO, tf)
    else:
        halo_g = hg_ref[...]
        halo_v = hv_ref[...]
    hb = h_scr[...]
    cw = cw_ref[0, f]
    cb = cb_ref[0, f]
    cg, last_g = conv(_dot(hb, wg_ref[0]), halo_g, cw[:, :tf], cb[:, :tf])
    cv, last_v = conv(_dot(hb, wv_ref[0]), halo_v, cw[:, tf:], cb[:, tf:])
    lg_ref[...] = last_g
    lv_ref[...] = last_v
    if carried:
        carry_g[f] = last_g[0]
        carry_v[f] = last_v[0]
    act = (_silu(cg) * cv).astype(BF16)
    xo_ref[...] += _dot(act, wd_ref[0])

    if final:
        @pl.when(f == NF - 1)
        def _():
            xo_ref[...] = _rms(xo_ref[...], fn_ref[...])


def _ffn_call(l, x, w, fnw, halos, *, batch, tm, n_seq, final):
    m = x.shape[0]
    carried = halos is None
    kern = functools.partial(_ffn_kernel, n_seq=n_seq, carried=carried, final=final)
    if carried:
        nt = m // (batch * tm)
        grid = (batch, nt, NF)
        row = lambda b, t, f: (b * nt + t, 0)
        fi = lambda b, t, f: f
        si = lambda b, t, f: b * nt + t
        n_state = batch * nt
    else:
        grid = (m // tm, NF)
        row = lambda i, f: (i, 0)
        fi = lambda i, f: f
        si = lambda i, f: i
        n_state = m // (tm // n_seq)
    in_specs = [
        _x_tile_spec(tm, row, tm > TM),
        pl.BlockSpec((1, 1, D_MODEL), lambda *a: (l, 0, 0)),
        pl.BlockSpec((1, D_MODEL, TF), lambda *a: (l, 0, fi(*a))),
        pl.BlockSpec((1, D_MODEL, TF), lambda *a: (l, 0, NF + fi(*a))),
        pl.BlockSpec((1, NF, FFN_CONV, 2 * TF), lambda *a: (l, 0, 0, 0)),
        pl.BlockSpec((1, NF, 1, 2 * TF), lambda *a: (l, 0, 0, 0)),
        pl.BlockSpec((1, TF, D_MODEL), lambda *a: (l, fi(*a), 0)),
        pl.BlockSpec((1, D_MODEL), lambda *a: (0, 0)),
    ]
    args = [x, w["n2"], w["w_up"], w["w_up"], w["ffn_cw"], w["ffn_cb"], w["w_down"], fnw]
    scratch = [pltpu.VMEM((tm, D_MODEL), BF16)]
    if carried:
        scratch += [pltpu.VMEM((NF, HALO, TF), F32), pltpu.VMEM((NF, HALO, TF), F32)]
    else:
        in_specs += [pl.BlockSpec((n_seq, HALO, TF), lambda *a: (si(*a), 0, fi(*a))),
                     pl.BlockSpec((n_seq, HALO, TF), lambda *a: (si(*a), 0, fi(*a) + NF))]
        args += [halos, halos]
    st_rows = 1 if carried else n_seq
    return pl.pallas_call(
        kern,
        out_shape=(jax.ShapeDtypeStruct((m, D_MODEL), F32),
                   jax.ShapeDtypeStruct((n_state, HALO, D_FF), F32),
                   jax.ShapeDtypeStruct((n_state, HALO, D_FF), F32)),
        grid=grid,
        in_specs=in_specs,
        out_specs=(
            pl.BlockSpec((tm, D_MODEL), row),
            pl.BlockSpec((st_rows, HALO, TF), lambda *a: (si(*a), 0, fi(*a))),
            pl.BlockSpec((st_rows, HALO, TF), lambda *a: (si(*a), 0, fi(*a))),
        ),
        scratch_shapes=scratch,
        compiler_params=pltpu.CompilerParams(
            dimension_semantics=("arbitrary",) * len(grid), vmem_limit_bytes=VMEM_LIMIT),
        name="ffn_prompt" if carried else "ffn_sample",
    )(*args)


def _group_cols(w, n_groups):
    return w.reshape(w.shape[:-1] + (n_groups, w.shape[-1] // n_groups))


def _xbc_to_groups(a):
    xs = _group_cols(a[..., :D_SSD], SSD_GROUPS)
    bs = _group_cols(a[..., D_SSD:D_SSD + SSD_GROUPS * D_STATE], SSD_GROUPS)
    cs = _group_cols(a[..., D_SSD + SSD_GROUPS * D_STATE:], SSD_GROUPS)
    return jnp.concatenate([xs, bs, cs], axis=-1)


def _groups_to_xbc(a):
    lead = a.shape[:-2]
    xs = a[..., :GROUP_W].reshape(lead + (D_SSD,))
    bs = a[..., GROUP_W:GROUP_W + D_STATE].reshape(lead + (SSD_GROUPS * D_STATE,))
    cs = a[..., GROUP_W + D_STATE:].reshape(lead + (SSD_GROUPS * D_STATE,))
    return jnp.concatenate([xs, bs, cs], axis=-1)


def _ffn_blocks(a, nf):
    return jnp.concatenate([_group_cols(a[..., :D_FF], nf), _group_cols(a[..., D_FF:], nf)], axis=-1)


def _pad_halo(state):
    return jnp.pad(state, ((0, 0), (HALO - state.shape[1], 0), (0, 0)))


def _weights(norm1_w, w_in, ssd_conv_w, ssd_conv_b, dt_bias, a_log, ssd_d, ssd_norm_w, gmlp_norm_w, w_out, norm2_w,
             w_up, ffn_conv_w, ffn_conv_b, w_down):
    d = w_in.shape[0]
    return dict(
        n1=norm1_w[:, None, :], n2=norm2_w[:, None, :],
        w_in=w_in.astype(BF16),
        w_uv=w_in[:, :, S_U:].astype(BF16),
        wo=w_out.astype(BF16).reshape(d, 2 * SSD_GROUPS, GROUP_W, D_MODEL),
        w_up=w_up.astype(BF16),
        w_down=w_down.astype(BF16),
        dtb=jnp.pad(dt_bias, ((0, 0), (0, LANES - SSD_HEADS)))[:, None, :],
        alog=jnp.pad(a_log, ((0, 0), (0, LANES - SSD_HEADS)))[:, None, :],
        ssd_cw=_xbc_to_groups(ssd_conv_w).transpose(0, 2, 1, 3),
        ssd_cb=_xbc_to_groups(ssd_conv_b)[:, :, None, :],
        dvec=jnp.repeat(ssd_d, SSD_HEAD_DIM, axis=-1).reshape(d, SSD_GROUPS, 1, GROUP_W),
        ssd_nw=ssd_norm_w.reshape(d, SSD_GROUPS, 1, GROUP_W),
        gnw=gmlp_norm_w.reshape(d, GMLP_GROUPS, 1, GROUP_W),
        ffn_cw=_ffn_blocks(ffn_conv_w, NF).transpose(0, 2, 1, 3),
        ffn_cb=_ffn_blocks(ffn_conv_b, NF)[:, :, None, :],
    )


def kernel(x_prompt, x_sample, state_ssd_conv, state_ssd, state_ffn_conv, norm1_w, w_in, ssd_conv_w, ssd_conv_b,
           dt_bias, a_log, ssd_d, ssd_norm_w, gmlp_norm_w, gmlp_w_s, gmlp_b_s, w_out, norm2_w, w_up, ffn_conv_w,
           ffn_conv_b, w_down, final_norm_w):
    bp, lp, _ = x_prompt.shape
    bs, ls, _ = x_sample.shape
    xp = x_prompt.reshape(bp * lp, D_MODEL)
    xs = x_sample.reshape(bs * ls, D_MODEL)
    ms = bs * ls
    fnw = final_norm_w[None, :]
    ssd_seqs = 16
    ntp = lp // TM
    w = _weights(norm1_w, w_in, ssd_conv_w, ssd_conv_b, dt_bias, a_log, ssd_d, ssd_norm_w, gmlp_norm_w, w_out,
                 norm2_w, w_up, ffn_conv_w, ffn_conv_b, w_down)
    ws_p = gmlp_w_s
    bs_p = gmlp_b_s[:, :, :, None]
    ws_s = jnp.tile(gmlp_w_s[:, :, :ls, :ls], (1, 1, 1, bs))
    bs_s = jnp.tile(gmlp_b_s[:, :, :ls], (1, 1, bs))[:, :, :, None]
    p_conv, p_ssd, p_ffn = [], [], []
    s_conv, s_ssd, s_ffn, s_v = [], [], [], []
    for l in range(DEPTH):
        final = l == DEPTH - 1
        xg, h = _gmlp_call(l, xp, w, ws_p, bs_p, tm=TM_WIDE, chunk=GMLP_CHUNK, seg_len=GMLP_CHUNK, emit_vn=False)
        x1, cst, hfin = _ssd_prompt_call(l, h, xg, w, batch=bp, tm=TM)
        xp, lg, lv = _ffn_call(l, x1, w, fnw, None, batch=bp, tm=TM_WIDE, n_seq=1, final=final)
        cst, hfin = (a.reshape((bp, ntp) + a.shape[1:])[:, -1] for a in (cst, hfin))
        lg, lv = (a.reshape((bp, lp // TM_WIDE) + a.shape[1:])[:, -1] for a in (lg, lv))
        p_conv.append(_groups_to_xbc(cst[:, :, HALO - (SSD_CONV - 1):, :].transpose(0, 2, 1, 3)))
        p_ssd.append(hfin)
        p_ffn.append(jnp.concatenate([lg, lv], axis=-1)[:, HALO - (FFN_CONV - 1):, :])
        xg, h, vn = _gmlp_call(l, xs, w, ws_s, bs_s, tm=ms, chunk=ms, seg_len=ls, emit_vn=True)
        halo = _pad_halo(_xbc_to_groups(state_ssd_conv[l]).transpose(2, 0, 1, 3).reshape(
            SSD_GROUPS * bs, SSD_CONV - 1, XBC_G)).reshape(SSD_GROUPS, bs, HALO, XBC_G)
        x1, cst, hfin = _ssd_sample_call(l, h, xg, w, halo, state_ssd, n_seq=ssd_seqs, lt=ls)
        xs, lg, lv = _ffn_call(l, x1, w, fnw, _pad_halo(state_ffn_conv[l]), batch=1, tm=ms, n_seq=bs, final=final)
        s_conv.append(_groups_to_xbc(cst[:, :, HALO - (SSD_CONV - 1):, :].transpose(1, 2, 0, 3)))
        s_ssd.append(hfin)
        s_ffn.append(jnp.concatenate([lg, lv], axis=-1)[:, HALO - (FFN_CONV - 1):, :])
        s_v.append(vn.reshape(bs, ls, D_GMLP))
    return (xp.reshape(bp, lp, D_MODEL), xs.reshape(bs, ls, D_MODEL),
            jnp.stack(p_conv), jnp.stack(p_ssd), jnp.stack(p_ffn),
            jnp.stack(s_conv), jnp.stack(s_ssd), jnp.stack(s_ffn), jnp.stack(s_v))
```

```python
import functools

import jax
import jax.numpy as jnp
from jax import lax
from jax.experimental import pallas as pl
from jax.experimental.pallas import tpu as pltpu

D_MODEL = 2048
DEPTH = 2
CHUNK = 64
D_SSD = 2048
SSD_HEAD_DIM = 64
SSD_HEADS = 32
SSD_GROUPS = 8
HEADS_PER_GROUP = 4
D_STATE = 128
SSD_CONV = 4
D_XBC = 4096
D_GMLP = 2048
GMLP_GROUPS = 8
GMLP_CHUNK = 128
D_FF = 5632
FFN_CONV = 3
EPS = 1e-6

LANES = 128
GROUP_W = 256
XBC_G = GROUP_W + 2 * D_STATE
SSD_COLS = GROUP_W + XBC_G
HALO = 8
TM = 512
TM_WIDE = 1024
SSD_Q = 256
TF = 512
NF = D_FF // TF
VMEM_LIMIT = 60 * 1024 * 1024

S_X = D_SSD
S_B = S_X + D_SSD
S_C = S_B + SSD_GROUPS * D_STATE
S_DT = S_C + SSD_GROUPS * D_STATE
S_U = S_DT + SSD_HEADS
S_V = S_U + D_GMLP

F32 = jnp.float32
BF16 = jnp.bfloat16


def _dot(a, b):
    return jnp.dot(a, b, preferred_element_type=F32)


def _dot_nt(a, b):
    return lax.dot_general(a, b, (((1,), (1,)), ((), ())), preferred_element_type=F32)


def _dot_tn(a, b):
    return lax.dot_general(a, b, (((0,), (0,)), ((), ())), preferred_element_type=F32)


def _rms(x, w):
    return x * lax.rsqrt(jnp.mean(x * x, axis=-1, keepdims=True) + EPS) * w


def _silu(x):
    return x * jax.nn.sigmoid(x)


def _softplus(x):
    return jnp.maximum(x, 0.0) + jnp.log1p(jnp.exp(-jnp.abs(x)))


def _seg_cumsum(x, seg_len):
    pos = lax.broadcasted_iota(jnp.int32, x.shape, 0) & (seg_len - 1)
    k = 1
    while k < seg_len:
        x = x + jnp.where(pos >= k, pltpu.roll(x, k, 0), 0.0)
        k *= 2
    return x


def _seg_total(x, seg_len):
    n = x.shape[0]
    pos = lax.broadcasted_iota(jnp.int32, x.shape, 0) & (seg_len - 1)
    k = 1
    while k < seg_len:
        x = x + jnp.where((pos & k) == 0, pltpu.roll(x, n - k, 0), pltpu.roll(x, k, 0))
        k *= 2
    return x


def _head_cols(v, width):
    n = v.shape[0]
    lane_head = lax.broadcasted_iota(jnp.int32, (n, width), 1) >> 6
    out = jnp.broadcast_to(v[:, 3:4], (n, width))
    for e in (2, 1, 0):
        out = jnp.where(lane_head == e, jnp.broadcast_to(v[:, e:e + 1], (n, width)), out)
    return out


def _head_rows(row):
    row_head = lax.broadcasted_iota(jnp.int32, (GROUP_W, D_STATE), 0) >> 6
    out = jnp.broadcast_to(row[:, 3:4], (GROUP_W, D_STATE))
    for e in (2, 1, 0):
        out = jnp.where(row_head == e, jnp.broadcast_to(row[:, e:e + 1], (GROUP_W, D_STATE)), out)
    return out


def _run_skewed(s, n, first, stage1, stage2):
    @pl.when(s == 0)
    def _():
        first()
        for piece in stage1(0):
            piece()

    @pl.when((s > 0) & (s < n))
    def _():
        stage2((s - 1) & 1, s - 1, stage1(s & 1))

    @pl.when(s == n)
    def _():
        stage2((n - 1) & 1, n - 1, [])


def _gmlp_group(g, h_scr, wu_ref, wv_ref, gnw_ref, ws_ref, bs_ref, wo_ref, xo_ref, vn_ref, *, chunk, seg_len):
    tm = h_scr.shape[0]
    seg_shift = seg_len.bit_length() - 1
    chunk_shift = CHUNK.bit_length() - 1
    val = {}

    def proj_u():
        val["u"] = jax.nn.gelu(_dot(h_scr[...], wu_ref[0]))

    def proj_v():
        vn = _rms(jax.nn.gelu(_dot(h_scr[...], wv_ref[0])), gnw_ref[0, g])
        if vn_ref is not None:
            vn_ref[...] = vn
        val["vnb"] = vn.astype(BF16)

    def gate():
        ii = lax.broadcasted_iota(jnp.int32, (chunk, chunk), 0)
        jj = lax.broadcasted_iota(jnp.int32, (chunk, chunk), 1)
        same_seq = (ii >> seg_shift) == (jj >> seg_shift)
        block_causal = ((jj & (seg_len - 1)) >> chunk_shift) <= ((ii & (seg_len - 1)) >> chunk_shift)
        w_rows = ws_ref[0, 0]
        w_full = w_rows if seg_len == chunk else jnp.concatenate([w_rows] * (chunk // seg_len), axis=0)
        wm = jnp.where(same_seq & block_causal, w_full, 0.0).astype(BF16)
        bias = bs_ref[0, g]
        parts = []
        for c in range(tm // chunk):
            sg = _dot(wm, val["vnb"][c * chunk:(c + 1) * chunk]) + bias
            parts.append(val["u"][c * chunk:(c + 1) * chunk] * sg)
        y = parts[0] if len(parts) == 1 else jnp.concatenate(parts, axis=0)
        xo_ref[...] += _dot(y.astype(BF16), wo_ref[0, 0])

    return [proj_u, proj_v, gate]


def _gmlp_kernel(x_ref, n1_ref, wu_ref, wv_ref, gnw_ref, ws_ref, bs_ref, wo_ref, xo_ref, h_ref, *rest,
                 chunk, seg_len, emit_vn):
    vn_ref = rest[0] if emit_vn else None
    h_scr = rest[-1]
    g = pl.program_id(1)

    @pl.when(g == 0)
    def _():
        x = x_ref[...]
        hb = _rms(x, n1_ref[0]).astype(BF16)
        h_scr[...] = hb
        h_ref[...] = hb
        xo_ref[...] = x

    for piece in _gmlp_group(g, h_scr, wu_ref, wv_ref, gnw_ref, ws_ref, bs_ref, wo_ref, xo_ref, vn_ref,
                             chunk=chunk, seg_len=seg_len):
        piece()


def _gmlp_tile_kernel(x_ref, n1_ref, w_ref, gnw_ref, ws_ref, bs_ref, wo_ref, xo_ref, h_ref, h_scr):
    tm = x_ref.shape[0]
    chunk = GMLP_CHUNK
    chunk_shift = CHUNK.bit_length() - 1
    x = x_ref[...]
    hb = _rms(x, n1_ref[0]).astype(BF16)
    h_scr[...] = hb
    h_ref[...] = hb
    xo_ref[...] = x
    ii = lax.broadcasted_iota(jnp.int32, (chunk, chunk), 0)
    jj = lax.broadcasted_iota(jnp.int32, (chunk, chunk), 1)
    block_causal = (jj >> chunk_shift) <= (ii >> chunk_shift)

    def project(g):
        u = jax.nn.gelu(_dot(h_scr[...], w_ref[0, :, g * GROUP_W:(g + 1) * GROUP_W]))
        v = jax.nn.gelu(_dot(h_scr[...], w_ref[0, :, D_GMLP + g * GROUP_W:D_GMLP + (g + 1) * GROUP_W]))
        return u, v

    def gate(g, u, v):
        vnb = _rms(v, gnw_ref[0, g]).astype(BF16)
        wm = jnp.where(block_causal, ws_ref[0, g], 0.0).astype(BF16)
        bias = bs_ref[0, g]
        parts = []
        for c in range(tm // chunk):
            sg = _dot(wm, vnb[c * chunk:(c + 1) * chunk]) + bias
            parts.append(u[c * chunk:(c + 1) * chunk] * sg)
        y = jnp.concatenate(parts, axis=0)
        xo_ref[...] += _dot(y.astype(BF16), wo_ref[0, g])

    uv = project(0)
    for g in range(GMLP_GROUPS):
        nxt = project(g + 1) if g + 1 < GMLP_GROUPS else None
        gate(g, *uv)
        uv = nxt


def _gmlp_tile_call(l, x, w, ws, bs, *, tm):
    m = x.shape[0]
    once = dict(pipeline_mode=pl.Buffered(1))
    return pl.pallas_call(
        _gmlp_tile_kernel,
        out_shape=(jax.ShapeDtypeStruct((m, D_MODEL), F32), jax.ShapeDtypeStruct((m, D_MODEL), BF16)),
        grid=(m // tm,),
        in_specs=[
            pl.BlockSpec((tm, D_MODEL), lambda i: (i, 0)),
            pl.BlockSpec((1, 1, D_MODEL), lambda i: (l, 0, 0)),
            pl.BlockSpec((1, D_MODEL, 2 * D_GMLP), lambda i: (l, 0, 0), **once),
            pl.BlockSpec((1, GMLP_GROUPS, 1, GROUP_W), lambda i: (l, 0, 0, 0)),
            pl.BlockSpec((1, GMLP_GROUPS, GMLP_CHUNK, GMLP_CHUNK), lambda i: (l, 0, 0, 0), **once),
            pl.BlockSpec((1, GMLP_GROUPS, GMLP_CHUNK, 1), lambda i: (l, 0, 0, 0), **once),
            pl.BlockSpec((1, GMLP_GROUPS, GROUP_W, D_MODEL), lambda i: (l, 1, 0, 0), **once),
        ],
        out_specs=(pl.BlockSpec((tm, D_MODEL), lambda i: (i, 0)), pl.BlockSpec((tm, D_MODEL), lambda i: (i, 0))),
        scratch_shapes=[pltpu.VMEM((tm, D_MODEL), BF16)],
        compiler_params=pltpu.CompilerParams(dimension_semantics=("arbitrary",), vmem_limit_bytes=VMEM_LIMIT),
        name="gmlp_tile",
    )(x, w["n1"], w["w_uv"], w["gnw"], ws, bs, w["wo"])


def _x_tile_spec(tm, index_map, single):
    if single:
        return pl.BlockSpec((tm, D_MODEL), index_map, pipeline_mode=pl.Buffered(1))
    return pl.BlockSpec((tm, D_MODEL), index_map)


def _gmlp_call(l, x, w, ws, bs, *, tm, chunk, seg_len, emit_vn):
    m = x.shape[0]
    nt = m // tm
    kern = functools.partial(_gmlp_kernel, chunk=chunk, seg_len=seg_len, emit_vn=emit_vn)
    out_shape = [jax.ShapeDtypeStruct((m, D_MODEL), F32), jax.ShapeDtypeStruct((m, D_MODEL), BF16)]
    out_specs = [pl.BlockSpec((tm, D_MODEL), lambda i, g: (i, 0)), pl.BlockSpec((tm, D_MODEL), lambda i, g: (i, 0))]
    if emit_vn:
        out_shape.append(jax.ShapeDtypeStruct((m, D_GMLP), F32))
        out_specs.append(pl.BlockSpec((tm, GROUP_W), lambda i, g: (i, g)))
    return pl.pallas_call(
        kern,
        out_shape=tuple(out_shape),
        grid=(nt, GMLP_GROUPS),
        in_specs=[
            _x_tile_spec(tm, lambda i, g: (i, 0), tm > TM),
            pl.BlockSpec((1, 1, D_MODEL), lambda i, g: (l, 0, 0)),
            pl.BlockSpec((1, D_MODEL, GROUP_W), lambda i, g: (l, 0, g)),
            pl.BlockSpec((1, D_MODEL, GROUP_W), lambda i, g: (l, 0, GMLP_GROUPS + g)),
            pl.BlockSpec((1, GMLP_GROUPS, 1, GROUP_W), lambda i, g: (l, 0, 0, 0)),
            pl.BlockSpec((1, 1, seg_len, chunk), lambda i, g: (l, g, 0, 0)),
            pl.BlockSpec((1, GMLP_GROUPS, chunk, 1), lambda i, g: (l, 0, 0, 0)),
            pl.BlockSpec((1, 1, GROUP_W, D_MODEL), lambda i, g: (l, SSD_GROUPS + g, 0, 0)),
        ],
        out_specs=tuple(out_specs),
        scratch_shapes=[pltpu.VMEM((tm, D_MODEL), BF16)],
        compiler_params=pltpu.CompilerParams(
            dimension_semantics=("arbitrary", "arbitrary"), vmem_limit_bytes=VMEM_LIMIT),
        name="gmlp_group",
    )(x, w["n1"], w["w_uv"], w["w_uv"], w["gnw"], ws, bs, w["wo"])


def _ssd_heads(h_ref, wdt_ref, dtb_ref, alog_ref, seg_len, hd):
    acs_scr, eacs_scr, dend_scr, etot_scr, acs_t_scr, dt_t_scr = hd
    dt = _softplus(_dot(h_ref[...], wdt_ref[0]) + dtb_ref[0])
    dta = dt * (-jnp.exp(alog_ref[0]))
    acs = _seg_cumsum(dta, seg_len)
    tot = _seg_total(dta, seg_len)
    acs_scr[...] = acs
    eacs_scr[...] = jnp.exp(acs)
    dend_scr[...] = jnp.exp(tot - acs) * dt
    etot_scr[...] = jnp.exp(tot)
    acs_t_scr[...] = acs.T
    dt_t_scr[...] = dt.T


def _group_lanes(v, g):
    return pltpu.roll(v, (LANES - HEADS_PER_GROUP * g) & (LANES - 1), 1)


def _ssd_stage1(h_ref, w_refs, pz_scr, pxbc_scr, slot):
    wz_ref, wx_ref, wb_ref, wc_ref = w_refs
    tm = h_ref.shape[0]
    rows = pl.ds(HALO, tm)

    def proj_z():
        pz_scr[slot] = _dot(h_ref[...], wz_ref[0])

    def proj_x():
        pxbc_scr[slot, rows, 0:GROUP_W] = _dot(h_ref[...], wx_ref[0])

    def proj_bc():
        w_bc = jnp.concatenate([wb_ref[0], wc_ref[0]], axis=1)
        pxbc_scr[slot, rows, GROUP_W:XBC_G] = _dot(h_ref[...], w_bc)

    return [proj_z, proj_x, proj_bc]


def _ssd_finish(g, y, xs, z, dvec_ref, nw_ref, wo_ref, xo_ref):
    y = y + dvec_ref[0, g] * xs
    y = y * _silu(z)
    y = _rms(y, nw_ref[0, g])
    xo_ref[...] += _dot(y.astype(BF16), wo_ref[0, 0])


def _intra(cb, acs, e, acs_row, dt_row, mask):
    seg = acs[:, e:e + 1] - acs_row
    if mask is not None:
        seg = jnp.where(mask, seg, -jnp.inf)
    return (cb * jnp.exp(seg) * dt_row).astype(BF16)


def _head_row(t_scr, g, e, c0, n):
    return t_scr[pl.ds(HEADS_PER_GROUP * g + e, 1), c0:c0 + n]


def _ssd_stage2_prompt(slot, g, tm, pz_scr, pxbc_scr, hd, carry, state, cw_ref, cb_ref,
                       dvec_ref, nw_ref, wo_ref, xo_ref, cst_ref, hfin_ref, fills, q):
    half = q // 2
    fills = list(fills)
    acs_scr, eacs_scr, dend_scr, etot_scr, acs_t_scr, dt_t_scr = hd
    if fills:
        fills.pop(0)()
    cw = cw_ref[0, g]
    pxbc_scr[slot, 0:HALO, :] = carry[g]
    acc = cb_ref[0, g] + pxbc_scr[slot, pl.ds(HALO - 3, tm), :] * cw[0:1]
    for k in (1, 2, 3):
        acc = acc + pxbc_scr[slot, pl.ds(HALO - 3 + k, tm), :] * cw[k:k + 1]
    last = pxbc_scr[slot, pl.ds(tm, HALO), :]
    carry[g] = last
    cst_ref[0, 0] = last
    xbc = _silu(acc)
    xs = xbc[:, :GROUP_W]
    xsb = xs.astype(BF16)
    bb = xbc[:, GROUP_W:GROUP_W + D_STATE].astype(BF16)
    cc = xbc[:, GROUP_W + D_STATE:].astype(BF16)
    ii = lax.broadcasted_iota(jnp.int32, (half, half), 0)
    jj = lax.broadcasted_iota(jnp.int32, (half, half), 1)
    tri = ii >= jj
    lane_head = lax.broadcasted_iota(jnp.int32, (q, GROUP_W), 1) >> 6
    hst = state[g]
    ys = []
    for c in range(tm // q):
        if fills:
            fills.pop(0)()
        r0 = c * q
        sl = slice(r0, r0 + q)
        acs = _group_lanes(acs_scr[sl], g)
        scale = _head_cols(_group_lanes(eacs_scr[sl], g), GROUP_W)
        dend = _head_cols(_group_lanes(dend_scr[sl], g), GROUP_W)
        etot = _group_lanes(etot_scr[r0:r0 + HALO], g)[0:1]
        cbm = _dot_nt(cc[sl], bb[sl])
        top, bot = None, None
        for e in range(HEADS_PER_GROUP):
            arow = _head_row(acs_t_scr, g, e, r0, q)
            drow = _head_row(dt_t_scr, g, e, r0, q)
            xe = jnp.where(lane_head == e, xsb[sl], jnp.zeros((q, GROUP_W), BF16))
            m00 = _intra(cbm[:half, :half], acs[:half], e, arow[:, :half], drow[:, :half], tri)
            m10 = _intra(cbm[half:, :half], acs[half:], e, arow[:, :half], drow[:, :half], None)
            m11 = _intra(cbm[half:, half:], acs[half:], e, arow[:, half:], drow[:, half:], tri)
            d0 = _dot(m00, xe[:half])
            d1 = _dot(jnp.concatenate([m10, m11], axis=1), xe)
            top = d0 if top is None else top + d0
            bot = d1 if bot is None else bot + d1
        ydiag = jnp.concatenate([top, bot], axis=0)
        yoff = _dot_nt(cc[sl], hst.astype(BF16)) * scale
        ys.append(ydiag + yoff)
        s_new = _dot_tn((xs[sl] * dend).astype(BF16), bb[sl])
        hst = hst * _head_rows(etot) + s_new
    for fill in fills:
        fill()
    state[g] = hst
    hfin_ref[0] = hst.reshape(HEADS_PER_GROUP, SSD_HEAD_DIM, D_STATE)
    y = ys[0] if len(ys) == 1 else jnp.concatenate(ys, axis=0)
    _ssd_finish(g, y, xs, pz_scr[slot], dvec_ref, nw_ref, wo_ref, xo_ref)


def _ssd_prompt_kernel(h_ref, xg_ref, wz_ref, wx_ref, wb_ref, wc_ref, wdt_ref, dtb_ref, alog_ref, cw_ref, cb_ref,
                       dvec_ref, nw_ref, wo_ref, xo_ref, cst_ref, hfin_ref, pz_scr, pxbc_scr, *scr, q):
    hd, (carry, state) = scr[:6], scr[6:]
    t = pl.program_id(1)
    s = pl.program_id(2)
    tm = h_ref.shape[0]

    @pl.when((s == 0) & (t == 0))
    def _():
        carry[...] = jnp.zeros(carry.shape, F32)
        state[...] = jnp.zeros(state.shape, F32)

    def first():
        xo_ref[...] = xg_ref[...]
        _ssd_heads(h_ref, wdt_ref, dtb_ref, alog_ref, q, hd)

    def stage1(slot):
        return _ssd_stage1(h_ref, (wz_ref, wx_ref, wb_ref, wc_ref), pz_scr, pxbc_scr, slot)

    def stage2(slot, g, fills):
        _ssd_stage2_prompt(slot, g, tm, pz_scr, pxbc_scr, hd, carry, state, cw_ref, cb_ref,
                           dvec_ref, nw_ref, wo_ref, xo_ref, cst_ref, hfin_ref, fills, q)

    _run_skewed(s, SSD_GROUPS, first, stage1, stage2)


def _ssd_sample_kernel(h_ref, xg_ref, wz_ref, wx_ref, wb_ref, wc_ref, wdt_ref, dtb_ref, alog_ref, cw_ref, cb_ref, dvec_ref, nw_ref, wo_ref,
                       halo_ref, h0_ref, xo_ref, cst_ref, hfin_ref, pz_scr, pxbc_scr, *hd, n_seq):
    acs_scr, eacs_scr, dend_scr, etot_scr, acs_t_scr, dt_t_scr = hd
    g = pl.program_id(1)
    tm = h_ref.shape[0]
    lt = tm // n_seq
    lt_shift = lt.bit_length() - 1

    @pl.when(g == 0)
    def _():
        xo_ref[...] = xg_ref[...]
        _ssd_heads(h_ref, wdt_ref, dtb_ref, alog_ref, lt, hd)

    for piece in _ssd_stage1(h_ref, (wz_ref, wx_ref, wb_ref, wc_ref), pz_scr, pxbc_scr, 0):
        piece()
    cw = cw_ref[0, g]
    pre3 = pxbc_scr[0, pl.ds(HALO, tm), :].reshape(n_seq, lt, XBC_G)
    xp = jnp.concatenate([halo_ref[0], pre3], axis=1)
    acc = cb_ref[0, g] + xp[:, HALO - 3:HALO - 3 + lt] * cw[0:1]
    for k in (1, 2, 3):
        acc = acc + xp[:, HALO - 3 + k:HALO - 3 + k + lt] * cw[k:k + 1]
    cst_ref[0] = pre3[:, lt - HALO:]
    xbc = _silu(acc.reshape(tm, XBC_G))
    xs = xbc[:, :GROUP_W]
    xsb = xs.astype(BF16)
    bb = xbc[:, GROUP_W:GROUP_W + D_STATE].astype(BF16)
    cc = xbc[:, GROUP_W + D_STATE:].astype(BF16)
    ii = lax.broadcasted_iota(jnp.int32, (tm, tm), 0)
    jj = lax.broadcasted_iota(jnp.int32, (tm, tm), 1)
    mask = ((ii >> lt_shift) == (jj >> lt_shift)) & (ii >= jj)
    acs = _group_lanes(acs_scr[...], g)
    scale = _head_cols(_group_lanes(eacs_scr[...], g), GROUP_W)
    dend = _head_cols(_group_lanes(dend_scr[...], g), GROUP_W)
    cd = _group_lanes(etot_scr[...], g)
    cbm = _dot_nt(cc, bb)
    lane_head = lax.broadcasted_iota(jnp.int32, (tm, GROUP_W), 1) >> 6
    ydiag = None
    for e in range(HEADS_PER_GROUP):
        mp = _intra(cbm, acs, e, _head_row(acs_t_scr, g, e, 0, tm), _head_row(dt_t_scr, g, e, 0, tm), mask)
        d = _dot(mp, jnp.where(lane_head == e, xsb, jnp.zeros((tm, GROUP_W), BF16)))
        ydiag = d if ydiag is None else ydiag + d
    xdtd = (xs * dend).astype(BF16)
    row_seq = lax.broadcasted_iota(jnp.int32, (tm, D_STATE), 0) >> lt_shift
    zero = jnp.zeros((tm, D_STATE), BF16)
    h0 = [h0_ref[0, i].reshape(GROUP_W, D_STATE) for i in range(n_seq)]
    h0_wide = jnp.concatenate(h0, axis=1).astype(BF16)
    c_wide = jnp.concatenate([jnp.where(row_seq == i, cc, zero) for i in range(n_seq)], axis=1)
    b_wide = jnp.concatenate([jnp.where(row_seq == i, bb, zero) for i in range(n_seq)], axis=1)
    yoff = _dot_nt(c_wide, h0_wide) * scale
    s_new = _dot_tn(xdtd, b_wide)
    for i in range(n_seq):
        hn = h0[i] * _head_rows(cd[i * lt:i * lt + 1, :]) + s_new[:, i * D_STATE:(i + 1) * D_STATE]
        hfin_ref[i] = hn.reshape(HEADS_PER_GROUP, SSD_HEAD_DIM, D_STATE)
    _ssd_finish(g, ydiag + yoff, xs, pz_scr[0], dvec_ref, nw_ref, wo_ref, xo_ref)


def _ssd_weight_specs(l, g1, g2):
    whole = lambda rows, width: pl.BlockSpec((1, SSD_GROUPS, rows, width), lambda *a: (l, 0, 0, 0))
    col = lambda width, start: pl.BlockSpec((1, D_MODEL, width), lambda *a: (l, 0, start // width + g1(*a)))
    return [
        col(GROUP_W, 0), col(GROUP_W, S_X), col(D_STATE, S_B), col(D_STATE, S_C),
        pl.BlockSpec((1, D_MODEL, LANES), lambda *a: (l, 0, S_DT // LANES)),
        pl.BlockSpec((1, 1, LANES), lambda *a: (l, 0, 0)),
        pl.BlockSpec((1, 1, LANES), lambda *a: (l, 0, 0)),
        whole(SSD_CONV, XBC_G), whole(1, XBC_G), whole(1, GROUP_W), whole(1, GROUP_W),
        pl.BlockSpec((1, 1, GROUP_W, D_MODEL), lambda *a: (l, g2(*a), 0, 0)),
    ]


def _ssd_weight_args(w):
    return [w["w_in"]] * 5 + [w["dtb"], w["alog"], w["ssd_cw"], w["ssd_cb"], w["dvec"], w["ssd_nw"], w["wo"]]


def _ssd_scratch(tm, slots):
    return ([pltpu.VMEM((slots, tm, GROUP_W), F32), pltpu.VMEM((slots, HALO + tm, XBC_G), F32)]
            + [pltpu.VMEM((tm, LANES), F32)] * 4 + [pltpu.VMEM((LANES, tm), F32)] * 2)


def _ssd_prompt_call(l, h, xg, w, *, batch, tm):
    m = h.shape[0]
    nt = m // (batch * tm)
    row = lambda b, t, s: (b * nt + t, 0)
    g1 = lambda b, t, s: jnp.minimum(s, SSD_GROUPS - 1)
    g2 = lambda b, t, s: jnp.maximum(s - 1, 0)
    return pl.pallas_call(
        functools.partial(_ssd_prompt_kernel, q=SSD_Q),
        out_shape=(jax.ShapeDtypeStruct((m, D_MODEL), F32),
                   jax.ShapeDtypeStruct((batch * nt, SSD_GROUPS, HALO, XBC_G), F32),
                   jax.ShapeDtypeStruct((batch * nt, SSD_HEADS, SSD_HEAD_DIM, D_STATE), F32)),
        grid=(batch, nt, SSD_GROUPS + 1),
        in_specs=[pl.BlockSpec((tm, D_MODEL), row), pl.BlockSpec((tm, D_MODEL), row)]
        + _ssd_weight_specs(l, g1, g2),
        out_specs=(
            pl.BlockSpec((tm, D_MODEL), row),
            pl.BlockSpec((1, 1, HALO, XBC_G), lambda b, t, s: (b * nt + t, g2(b, t, s), 0, 0)),
            pl.BlockSpec((1, HEADS_PER_GROUP, SSD_HEAD_DIM, D_STATE),
                         lambda b, t, s: (b * nt + t, g2(b, t, s), 0, 0)),
        ),
        scratch_shapes=_ssd_scratch(tm, 2)
        + [pltpu.VMEM((SSD_GROUPS, HALO, XBC_G), F32), pltpu.VMEM((SSD_GROUPS, GROUP_W, D_STATE), F32)],
        compiler_params=pltpu.CompilerParams(
            dimension_semantics=("arbitrary", "arbitrary", "arbitrary"), vmem_limit_bytes=VMEM_LIMIT),
        name="ssd_prompt_group",
    )(h, xg, *_ssd_weight_args(w))


def _ssd_sample_call(l, h, xg, w, halo, h0, *, n_seq, lt):
    m = h.shape[0]
    tm = n_seq * lt
    nt = m // tm
    n_all = m // lt
    kern = functools.partial(_ssd_sample_kernel, n_seq=n_seq)
    row = lambda i, g: (i, 0)
    gi = lambda i, g: g
    return pl.pallas_call(
        kern,
        out_shape=(jax.ShapeDtypeStruct((m, D_MODEL), F32),
                   jax.ShapeDtypeStruct((SSD_GROUPS, n_all, HALO, XBC_G), F32),
                   jax.ShapeDtypeStruct((n_all, SSD_HEADS, SSD_HEAD_DIM, D_STATE), F32)),
        grid=(nt, SSD_GROUPS),
        in_specs=[pl.BlockSpec((tm, D_MODEL), row), pl.BlockSpec((tm, D_MODEL), row)]
        + _ssd_weight_specs(l, gi, gi)
        + [pl.BlockSpec((1, n_seq, HALO, XBC_G), lambda i, g: (g, i, 0, 0)),
           pl.BlockSpec((1, n_seq, HEADS_PER_GROUP, SSD_HEAD_DIM, D_STATE), lambda i, g: (l, i, g, 0, 0))],
        out_specs=(
            pl.BlockSpec((tm, D_MODEL), row),
            pl.BlockSpec((1, n_seq, HALO, XBC_G), lambda i, g: (g, i, 0, 0)),
            pl.BlockSpec((n_seq, HEADS_PER_GROUP, SSD_HEAD_DIM, D_STATE), lambda i, g: (i, g, 0, 0)),
        ),
        scratch_shapes=_ssd_scratch(tm, 1),
        compiler_params=pltpu.CompilerParams(
            dimension_semantics=("arbitrary", "arbitrary"), vmem_limit_bytes=VMEM_LIMIT),
        name="ssd_sample_group",
    )(h, xg, *_ssd_weight_args(w), halo, h0)


def _ffn_kernel(*refs, n_seq, carried, final):
    if carried:
        (x_ref, n2_ref, wg_ref, wv_ref, cw_ref, cb_ref, wd_ref, fn_ref,
         xo_ref, lg_ref, lv_ref, h_scr, carry_g, carry_v) = refs
    else:
        (x_ref, n2_ref, wg_ref, wv_ref, cw_ref, cb_ref, wd_ref, fn_ref, hg_ref, hv_ref,
         xo_ref, lg_ref, lv_ref, h_scr) = refs
    nax = 3 if carried else 2
    f = pl.program_id(nax - 1)
    tm = x_ref.shape[0]
    lt = tm // n_seq
    tf = TF

    @pl.when(f == 0)
    def _():
        x = x_ref[...]
        h_scr[...] = _rms(x, n2_ref[0]).astype(BF16)
        xo_ref[...] = x

    def conv(u, halo, cw, cb):
        if n_seq == 1:
            xp = jnp.concatenate([halo[0], u], axis=0)
            acc = cb + xp[HALO - 2:HALO - 2 + tm] * cw[0:1]
            acc = acc + xp[HALO - 1:HALO - 1 + tm] * cw[1:2]
            acc = acc + xp[HALO:] * cw[2:3]
            return acc, u[tm - HALO:].reshape(1, HALO, tf)
        u3 = u.reshape(n_seq, lt, tf)
        xp = jnp.concatenate([halo, u3], axis=1)
        acc = cb + xp[:, HALO - 2:HALO - 2 + lt] * cw[0:1]
        acc = acc + xp[:, HALO - 1:HALO - 1 + lt] * cw[1:2]
        acc = acc + xp[:, HALO:] * cw[2:3]
        return acc.reshape(tm, tf), u3[:, lt - HALO:]

    if carried:
        @pl.when((f == 0) & (pl.program_id(1) == 0))
        def _():
            carry_g[...] = jnp.zeros(carry_g.shape, F32)
            carry_v[...] = jnp.zeros(carry_v.shape, F32)
        halo_g = carry_g[f].reshape(1, HALO, tf)
        halo_v = carry_v[f].reshape(1, HALO, tf)
    else:
        halo_g = hg_ref[...]
        halo_v = hv_ref[...]
    hb = h_scr[...]
    cw = cw_ref[0, f]
    cb = cb_ref[0, f]
    cg, last_g = conv(_dot(hb, wg_ref[0]), halo_g, cw[:, :tf], cb[:, :tf])
    cv, last_v = conv(_dot(hb, wv_ref[0]), halo_v, cw[:, tf:], cb[:, tf:])
    lg_ref[...] = last_g
    lv_ref[...] = last_v
    if carried:
        carry_g[f] = last_g[0]
        carry_v[f] = last_v[0]
    act = (_silu(cg) * cv).astype(BF16)
    xo_ref[...] += _dot(act, wd_ref[0])

    if final:
        @pl.when(f == NF - 1)
        def _():
            xo_ref[...] = _rms(xo_ref[...], fn_ref[...])


def _ffn_call(l, x, w, fnw, halos, *, batch, tm, n_seq, final):
    m = x.shape[0]
    carried = halos is None
    kern = functools.partial(_ffn_kernel, n_seq=n_seq, carried=carried, final=final)
    if carried:
        nt = m // (batch * tm)
        grid = (batch, nt, NF)
        row = lambda b, t, f: (b * nt + t, 0)
        fi = lambda b, t, f: f
        si = lambda b, t, f: b * nt + t
        n_state = batch * nt
    else:
        grid = (m // tm, NF)
        row = lambda i, f: (i, 0)
        fi = lambda i, f: f
        si = lambda i, f: i
        n_state = m // (tm // n_seq)
    in_specs = [
        _x_tile_spec(tm, row, tm > TM),
        pl.BlockSpec((1, 1, D_MODEL), lambda *a: (l, 0, 0)),
        pl.BlockSpec((1, D_MODEL, TF), lambda *a: (l, 0, fi(*a))),
        pl.BlockSpec((1, D_MODEL, TF), lambda *a: (l, 0, NF + fi(*a))),
        pl.BlockSpec((1, NF, FFN_CONV, 2 * TF), lambda *a: (l, 0, 0, 0)),
        pl.BlockSpec((1, NF, 1, 2 * TF), lambda *a: (l, 0, 0, 0)),
        pl.BlockSpec((1, TF, D_MODEL), lambda *a: (l, fi(*a), 0)),
        pl.BlockSpec((1, D_MODEL), lambda *a: (0, 0)),
    ]
    args = [x, w["n2"], w["w_up"], w["w_up"], w["ffn_cw"], w["ffn_cb"], w["w_down"], fnw]
    scratch = [pltpu.VMEM((tm, D_MODEL), BF16)]
    if carried:
        scratch += [pltpu.VMEM((NF, HALO, TF), F32), pltpu.VMEM((NF, HALO, TF), F32)]
    else:
        in_specs += [pl.BlockSpec((n_seq, HALO, TF), lambda *a: (si(*a), 0, fi(*a))),
                     pl.BlockSpec((n_seq, HALO, TF), lambda *a: (si(*a), 0, fi(*a) + NF))]
        args += [halos, halos]
    st_rows = 1 if carried else n_seq
    return pl.pallas_call(
        kern,
        out_shape=(jax.ShapeDtypeStruct((m, D_MODEL), F32),
                   jax.ShapeDtypeStruct((n_state, HALO, D_FF), F32),
                   jax.ShapeDtypeStruct((n_state, HALO, D_FF), F32)),
        grid=grid,
        in_specs=in_specs,
        out_specs=(
            pl.BlockSpec((tm, D_MODEL), row),
            pl.BlockSpec((st_rows, HALO, TF), lambda *a: (si(*a), 0, fi(*a))),
            pl.BlockSpec((st_rows, HALO, TF), lambda *a: (si(*a), 0, fi(*a))),
        ),
        scratch_shapes=scratch,
        compiler_params=pltpu.CompilerParams(
            dimension_semantics=("arbitrary",) * len(grid), vmem_limit_bytes=VMEM_LIMIT),
        name="ffn_prompt" if carried else "ffn_sample",
    )(*args)


def _group_cols(w, n_groups):
    return w.reshape(w.shape[:-1] + (n_groups, w.shape[-1] // n_groups))


def _xbc_to_groups(a):
    xs = _group_cols(a[..., :D_SSD], SSD_GROUPS)
    bs = _group_cols(a[..., D_SSD:D_SSD + SSD_GROUPS * D_STATE], SSD_GROUPS)
    cs = _group_cols(a[..., D_SSD + SSD_GROUPS * D_STATE:], SSD_GROUPS)
    return jnp.concatenate([xs, bs, cs], axis=-1)


def _groups_to_xbc(a):
    lead = a.shape[:-2]
    xs = a[..., :GROUP_W].reshape(lead + (D_SSD,))
    bs = a[..., GROUP_W:GROUP_W + D_STATE].reshape(lead + (SSD_GROUPS * D_STATE,))
    cs = a[..., GROUP_W + D_STATE:].reshape(lead + (SSD_GROUPS * D_STATE,))
    return jnp.concatenate([xs, bs, cs], axis=-1)


def _ffn_blocks(a, nf):
    return jnp.concatenate([_group_cols(a[..., :D_FF], nf), _group_cols(a[..., D_FF:], nf)], axis=-1)


def _pad_halo(state):
    return jnp.pad(state, ((0, 0), (HALO - state.shape[1], 0), (0, 0)))


def _weights(norm1_w, w_in, ssd_conv_w, ssd_conv_b, dt_bias, a_log, ssd_d, ssd_norm_w, gmlp_norm_w, w_out, norm2_w,
             w_up, ffn_conv_w, ffn_conv_b, w_down):
    d = w_in.shape[0]
    return dict(
        n1=norm1_w[:, None, :], n2=norm2_w[:, None, :],
        w_in=w_in.astype(BF16),
        w_uv=w_in[:, :, S_U:].astype(BF16),
        wo=w_out.astype(BF16).reshape(d, 2 * SSD_GROUPS, GROUP_W, D_MODEL),
        w_up=w_up.astype(BF16),
        w_down=w_down.astype(BF16),
        dtb=jnp.pad(dt_bias, ((0, 0), (0, LANES - SSD_HEADS)))[:, None, :],
        alog=jnp.pad(a_log, ((0, 0), (0, LANES - SSD_HEADS)))[:, None, :],
        ssd_cw=_xbc_to_groups(ssd_conv_w).transpose(0, 2, 1, 3),
        ssd_cb=_xbc_to_groups(ssd_conv_b)[:, :, None, :],
        dvec=jnp.repeat(ssd_d, SSD_HEAD_DIM, axis=-1).reshape(d, SSD_GROUPS, 1, GROUP_W),
        ssd_nw=ssd_norm_w.reshape(d, SSD_GROUPS, 1, GROUP_W),
        gnw=gmlp_norm_w.reshape(d, GMLP_GROUPS, 1, GROUP_W),
        ffn_cw=_ffn_blocks(ffn_conv_w, NF).transpose(0, 2, 1, 3),
        ffn_cb=_ffn_blocks(ffn_conv_b, NF)[:, :, None, :],
    )


def kernel(x_prompt, x_sample, state_ssd_conv, state_ssd, state_ffn_conv, norm1_w, w_in, ssd_conv_w, ssd_conv_b,
           dt_bias, a_log, ssd_d, ssd_norm_w, gmlp_norm_w, gmlp_w_s, gmlp_b_s, w_out, norm2_w, w_up, ffn_conv_w,
           ffn_conv_b, w_down, final_norm_w):
    bp, lp, _ = x_prompt.shape
    bs, ls, _ = x_sample.shape
    xp = x_prompt.reshape(bp * lp, D_MODEL)
    xs = x_sample.reshape(bs * ls, D_MODEL)
    ms = bs * ls
    fnw = final_norm_w[None, :]
    ssd_seqs = 16
    ntp = lp // TM
    w = _weights(norm1_w, w_in, ssd_conv_w, ssd_conv_b, dt_bias, a_log, ssd_d, ssd_norm_w, gmlp_norm_w, w_out,
                 norm2_w, w_up, ffn_conv_w, ffn_conv_b, w_down)
    ws_p = gmlp_w_s
    bs_p = gmlp_b_s[:, :, :, None]
    ws_s = jnp.tile(gmlp_w_s[:, :, :ls, :ls], (1, 1, 1, bs))
    bs_s = jnp.tile(gmlp_b_s[:, :, :ls], (1, 1, bs))[:, :, :, None]
    p_conv, p_ssd, p_ffn = [], [], []
    s_conv, s_ssd, s_ffn, s_v = [], [], [], []
    for l in range(DEPTH):
        final = l == DEPTH - 1
        xg, h = _gmlp_tile_call(l, xp, w, ws_p, bs_p, tm=TM)
        x1, cst, hfin = _ssd_prompt_call(l, h, xg, w, batch=bp, tm=TM)
        xp, lg, lv = _ffn_call(l, x1, w, fnw, None, batch=bp, tm=TM_WIDE, n_seq=1, final=final)
        cst, hfin = (a.reshape((bp, ntp) + a.shape[1:])[:, -1] for a in (cst, hfin))
        lg, lv = (a.reshape((bp, lp // TM_WIDE) + a.shape[1:])[:, -1] for a in (lg, lv))
        p_conv.append(_groups_to_xbc(cst[:, :, HALO - (SSD_CONV - 1):, :].transpose(0, 2, 1, 3)))
        p_ssd.append(hfin)
        p_ffn.append(jnp.concatenate([lg, lv], axis=-1)[:, HALO - (FFN_CONV - 1):, :])
        xg, h, vn = _gmlp_call(l, xs, w, ws_s, bs_s, tm=ms, chunk=ms, seg_len=ls, emit_vn=True)
        halo = _pad_halo(_xbc_to_groups(state_ssd_conv[l]).transpose(2, 0, 1, 3).reshape(
            SSD_GROUPS * bs, SSD_CONV - 1, XBC_G)).reshape(SSD_GROUPS, bs, HALO, XBC_G)
        x1, cst, hfin = _ssd_sample_call(l, h, xg, w, halo, state_ssd, n_seq=ssd_seqs, lt=ls)
        xs, lg, lv = _ffn_call(l, x1, w, fnw, _pad_halo(state_ffn_conv[l]), batch=1, tm=ms, n_seq=bs, final=final)
        s_conv.append(_groups_to_xbc(cst[:, :, HALO - (SSD_CONV - 1):, :].transpose(1, 2, 0, 3)))
        s_ssd.append(hfin)
        s_ffn.append(jnp.concatenate([lg, lv], axis=-1)[:, HALO - (FFN_CONV - 1):, :])
        s_v.append(vn.reshape(bs, ls, D_GMLP))
    return (xp.reshape(bp, lp, D_MODEL), xs.reshape(bs, ls, D_MODEL),
            jnp.stack(p_conv), jnp.stack(p_ssd), jnp.stack(p_ffn),
            jnp.stack(s_conv), jnp.stack(s_ssd), jnp.stack(s_ffn), jnp.stack(s_v))
```

```python
import functools

import jax
import jax.numpy as jnp
from jax import lax
from jax.experimental import pallas as pl
from jax.experimental.pallas import tpu as pltpu

D_MODEL = 2048
DEPTH = 2
CHUNK = 64
D_SSD = 2048
SSD_HEAD_DIM = 64
SSD_HEADS = 32
SSD_GROUPS = 8
HEADS_PER_GROUP = 4
D_STATE = 128
SSD_CONV = 4
D_XBC = 4096
D_GMLP = 2048
GMLP_GROUPS = 8
GMLP_CHUNK = 128
D_FF = 5632
FFN_CONV = 3
EPS = 1e-6

LANES = 128
GROUP_W = 256
XBC_G = GROUP_W + 2 * D_STATE
SSD_COLS = GROUP_W + XBC_G
HALO = 8
TM = 512
TM_WIDE = 1024
SSD_Q = 256
PAIR = 2
TF = 512
NF = D_FF // TF
VMEM_LIMIT = 60 * 1024 * 1024

S_X = D_SSD
S_B = S_X + D_SSD
S_C = S_B + SSD_GROUPS * D_STATE
S_DT = S_C + SSD_GROUPS * D_STATE
S_U = S_DT + SSD_HEADS
S_V = S_U + D_GMLP

F32 = jnp.float32
BF16 = jnp.bfloat16


def _dot(a, b):
    return jnp.dot(a, b, preferred_element_type=F32)


def _dot_nt(a, b):
    return lax.dot_general(a, b, (((1,), (1,)), ((), ())), preferred_element_type=F32)


def _dot_tn(a, b):
    return lax.dot_general(a, b, (((0,), (0,)), ((), ())), preferred_element_type=F32)


def _rms(x, w):
    return x * lax.rsqrt(jnp.mean(x * x, axis=-1, keepdims=True) + EPS) * w


def _silu(x):
    return x * jax.nn.sigmoid(x)


def _softplus(x):
    return jnp.maximum(x, 0.0) + jnp.log1p(jnp.exp(-jnp.abs(x)))


def _seg_cumsum(x, seg_len):
    pos = lax.broadcasted_iota(jnp.int32, x.shape, 0) & (seg_len - 1)
    k = 1
    while k < seg_len:
        x = x + jnp.where(pos >= k, pltpu.roll(x, k, 0), 0.0)
        k *= 2
    return x


def _seg_total(x, seg_len):
    n = x.shape[0]
    pos = lax.broadcasted_iota(jnp.int32, x.shape, 0) & (seg_len - 1)
    k = 1
    while k < seg_len:
        x = x + jnp.where((pos & k) == 0, pltpu.roll(x, n - k, 0), pltpu.roll(x, k, 0))
        k *= 2
    return x


def _head_cols(v, width):
    n = v.shape[0]
    lane_head = lax.broadcasted_iota(jnp.int32, (n, width), 1) >> 6
    out = jnp.broadcast_to(v[:, 3:4], (n, width))
    for e in (2, 1, 0):
        out = jnp.where(lane_head == e, jnp.broadcast_to(v[:, e:e + 1], (n, width)), out)
    return out


def _head_rows(row):
    row_head = lax.broadcasted_iota(jnp.int32, (GROUP_W, D_STATE), 0) >> 6
    out = jnp.broadcast_to(row[:, 3:4], (GROUP_W, D_STATE))
    for e in (2, 1, 0):
        out = jnp.where(row_head == e, jnp.broadcast_to(row[:, e:e + 1], (GROUP_W, D_STATE)), out)
    return out


def _gmlp_group(g, h_scr, wu_ref, wv_ref, gnw_ref, ws_ref, bs_ref, wo_ref, xo_ref, vn_ref, *, chunk, seg_len):
    tm = h_scr.shape[0]
    seg_shift = seg_len.bit_length() - 1
    chunk_shift = CHUNK.bit_length() - 1
    val = {}

    def proj_u():
        val["u"] = jax.nn.gelu(_dot(h_scr[...], wu_ref[0]))

    def proj_v():
        vn = _rms(jax.nn.gelu(_dot(h_scr[...], wv_ref[0])), gnw_ref[0, g])
        if vn_ref is not None:
            vn_ref[...] = vn
        val["vnb"] = vn.astype(BF16)

    def gate():
        ii = lax.broadcasted_iota(jnp.int32, (chunk, chunk), 0)
        jj = lax.broadcasted_iota(jnp.int32, (chunk, chunk), 1)
        same_seq = (ii >> seg_shift) == (jj >> seg_shift)
        block_causal = ((jj & (seg_len - 1)) >> chunk_shift) <= ((ii & (seg_len - 1)) >> chunk_shift)
        w_rows = ws_ref[0, 0]
        w_full = w_rows if seg_len == chunk else jnp.concatenate([w_rows] * (chunk // seg_len), axis=0)
        wm = jnp.where(same_seq & block_causal, w_full, 0.0).astype(BF16)
        bias = bs_ref[0, g]
        parts = []
        for c in range(tm // chunk):
            sg = _dot(wm, val["vnb"][c * chunk:(c + 1) * chunk]) + bias
            parts.append(val["u"][c * chunk:(c + 1) * chunk] * sg)
        y = parts[0] if len(parts) == 1 else jnp.concatenate(parts, axis=0)
        xo_ref[...] += _dot(y.astype(BF16), wo_ref[0, 0])

    return [proj_u, proj_v, gate]


def _gmlp_kernel(x_ref, n1_ref, wu_ref, wv_ref, gnw_ref, ws_ref, bs_ref, wo_ref, xo_ref, h_ref, *rest,
                 chunk, seg_len, emit_vn):
    vn_ref = rest[0] if emit_vn else None
    h_scr = rest[-1]
    g = pl.program_id(1)

    @pl.when(g == 0)
    def _():
        x = x_ref[...]
        hb = _rms(x, n1_ref[0]).astype(BF16)
        h_scr[...] = hb
        h_ref[...] = hb
        xo_ref[...] = x

    for piece in _gmlp_group(g, h_scr, wu_ref, wv_ref, gnw_ref, ws_ref, bs_ref, wo_ref, xo_ref, vn_ref,
                             chunk=chunk, seg_len=seg_len):
        piece()


def _gmlp_tile_kernel(x_ref, n1_ref, w_ref, gnw_ref, ws_ref, bs_ref, wo_ref, xo_ref, h_ref, h_scr):
    tm = x_ref.shape[0]
    chunk = GMLP_CHUNK
    chunk_shift = CHUNK.bit_length() - 1
    x = x_ref[...]
    hb = _rms(x, n1_ref[0]).astype(BF16)
    h_scr[...] = hb
    h_ref[...] = hb
    xo_ref[...] = x
    ii = lax.broadcasted_iota(jnp.int32, (chunk, chunk), 0)
    jj = lax.broadcasted_iota(jnp.int32, (chunk, chunk), 1)
    block_causal = (jj >> chunk_shift) <= (ii >> chunk_shift)

    def project(g):
        u = jax.nn.gelu(_dot(h_scr[...], w_ref[0, :, g * GROUP_W:(g + 1) * GROUP_W]))
        v = jax.nn.gelu(_dot(h_scr[...], w_ref[0, :, D_GMLP + g * GROUP_W:D_GMLP + (g + 1) * GROUP_W]))
        return u, v

    def gate(g, u, v):
        vnb = _rms(v, gnw_ref[0, g]).astype(BF16)
        wm = jnp.where(block_causal, ws_ref[0, g], 0.0).astype(BF16)
        bias = bs_ref[0, g]
        parts = []
        for c in range(tm // chunk):
            sg = _dot(wm, vnb[c * chunk:(c + 1) * chunk]) + bias
            parts.append(u[c * chunk:(c + 1) * chunk] * sg)
        y = jnp.concatenate(parts, axis=0)
        xo_ref[...] += _dot(y.astype(BF16), wo_ref[0, g])

    uv = project(0)
    for g in range(GMLP_GROUPS):
        nxt = project(g + 1) if g + 1 < GMLP_GROUPS else None
        gate(g, *uv)
        uv = nxt


def _gmlp_tile_call(l, x, w, ws, bs, *, tm):
    m = x.shape[0]
    once = dict(pipeline_mode=pl.Buffered(1))
    return pl.pallas_call(
        _gmlp_tile_kernel,
        out_shape=(jax.ShapeDtypeStruct((m, D_MODEL), F32), jax.ShapeDtypeStruct((m, D_MODEL), BF16)),
        grid=(m // tm,),
        in_specs=[
            pl.BlockSpec((tm, D_MODEL), lambda i: (i, 0)),
            pl.BlockSpec((1, 1, D_MODEL), lambda i: (l, 0, 0)),
            pl.BlockSpec((1, D_MODEL, 2 * D_GMLP), lambda i: (l, 0, 0), **once),
            pl.BlockSpec((1, GMLP_GROUPS, 1, GROUP_W), lambda i: (l, 0, 0, 0)),
            pl.BlockSpec((1, GMLP_GROUPS, GMLP_CHUNK, GMLP_CHUNK), lambda i: (l, 0, 0, 0), **once),
            pl.BlockSpec((1, GMLP_GROUPS, GMLP_CHUNK, 1), lambda i: (l, 0, 0, 0), **once),
            pl.BlockSpec((1, GMLP_GROUPS, GROUP_W, D_MODEL), lambda i: (l, 1, 0, 0), **once),
        ],
        out_specs=(pl.BlockSpec((tm, D_MODEL), lambda i: (i, 0)), pl.BlockSpec((tm, D_MODEL), lambda i: (i, 0))),
        scratch_shapes=[pltpu.VMEM((tm, D_MODEL), BF16)],
        compiler_params=pltpu.CompilerParams(dimension_semantics=("arbitrary",), vmem_limit_bytes=VMEM_LIMIT),
        name="gmlp_tile",
    )(x, w["n1"], w["w_uv"], w["gnw"], ws, bs, w["wo"])


def _x_tile_spec(tm, index_map, single):
    if single:
        return pl.BlockSpec((tm, D_MODEL), index_map, pipeline_mode=pl.Buffered(1))
    return pl.BlockSpec((tm, D_MODEL), index_map)


def _gmlp_call(l, x, w, ws, bs, *, tm, chunk, seg_len, emit_vn):
    m = x.shape[0]
    nt = m // tm
    kern = functools.partial(_gmlp_kernel, chunk=chunk, seg_len=seg_len, emit_vn=emit_vn)
    out_shape = [jax.ShapeDtypeStruct((m, D_MODEL), F32), jax.ShapeDtypeStruct((m, D_MODEL), BF16)]
    out_specs = [pl.BlockSpec((tm, D_MODEL), lambda i, g: (i, 0)), pl.BlockSpec((tm, D_MODEL), lambda i, g: (i, 0))]
    if emit_vn:
        out_shape.append(jax.ShapeDtypeStruct((m, D_GMLP), F32))
        out_specs.append(pl.BlockSpec((tm, GROUP_W), lambda i, g: (i, g)))
    return pl.pallas_call(
        kern,
        out_shape=tuple(out_shape),
        grid=(nt, GMLP_GROUPS),
        in_specs=[
            _x_tile_spec(tm, lambda i, g: (i, 0), tm > TM),
            pl.BlockSpec((1, 1, D_MODEL), lambda i, g: (l, 0, 0)),
            pl.BlockSpec((1, D_MODEL, GROUP_W), lambda i, g: (l, 0, g)),
            pl.BlockSpec((1, D_MODEL, GROUP_W), lambda i, g: (l, 0, GMLP_GROUPS + g)),
            pl.BlockSpec((1, GMLP_GROUPS, 1, GROUP_W), lambda i, g: (l, 0, 0, 0)),
            pl.BlockSpec((1, 1, seg_len, chunk), lambda i, g: (l, g, 0, 0)),
            pl.BlockSpec((1, GMLP_GROUPS, chunk, 1), lambda i, g: (l, 0, 0, 0)),
            pl.BlockSpec((1, 1, GROUP_W, D_MODEL), lambda i, g: (l, SSD_GROUPS + g, 0, 0)),
        ],
        out_specs=tuple(out_specs),
        scratch_shapes=[pltpu.VMEM((tm, D_MODEL), BF16)],
        compiler_params=pltpu.CompilerParams(
            dimension_semantics=("arbitrary", "arbitrary"), vmem_limit_bytes=VMEM_LIMIT),
        name="gmlp_group",
    )(x, w["n1"], w["w_uv"], w["w_uv"], w["gnw"], ws, bs, w["wo"])


def _ssd_heads(h_ref, wdt_ref, dtb_ref, alog_ref, seg_len, hd):
    acs_scr, eacs_scr, dend_scr, etot_scr, acs_t_scr, dt_t_scr = hd
    dt = _softplus(_dot(h_ref[...], wdt_ref[0]) + dtb_ref[0])
    dta = dt * (-jnp.exp(alog_ref[0]))
    acs = _seg_cumsum(dta, seg_len)
    tot = _seg_total(dta, seg_len)
    acs_scr[...] = acs
    eacs_scr[...] = jnp.exp(acs)
    dend_scr[...] = jnp.exp(tot - acs) * dt
    etot_scr[...] = jnp.exp(tot)
    acs_t_scr[...] = acs.T
    dt_t_scr[...] = dt.T


def _group_lanes(v, g):
    return pltpu.roll(v, (LANES - HEADS_PER_GROUP * g) & (LANES - 1), 1)


def _ssd_stage1(h_ref, w_refs, pz_scr, pxbc_scr, slot):
    wz_ref, wx_ref, wb_ref, wc_ref = w_refs
    tm = h_ref.shape[0]
    rows = pl.ds(HALO, tm)

    def proj_z():
        pz_scr[slot] = _dot(h_ref[...], wz_ref[0])

    def proj_x():
        pxbc_scr[slot, rows, 0:GROUP_W] = _dot(h_ref[...], wx_ref[0])

    def proj_bc():
        w_bc = jnp.concatenate([wb_ref[0], wc_ref[0]], axis=1)
        pxbc_scr[slot, rows, GROUP_W:XBC_G] = _dot(h_ref[...], w_bc)

    return [proj_z, proj_x, proj_bc]


def _ssd_finish(g, y, xs, z, dvec_ref, nw_ref, wo_ref, xo_ref, j=0):
    y = y + dvec_ref[0, g] * xs
    y = y * _silu(z)
    y = _rms(y, nw_ref[0, g])
    xo_ref[...] += _dot(y.astype(BF16), wo_ref[0, j])


def _intra(cb, acs, e, acs_row, dt_row, mask):
    seg = acs[:, e:e + 1] - acs_row
    if mask is not None:
        seg = jnp.where(mask, seg, -jnp.inf)
    return (cb * jnp.exp(seg) * dt_row).astype(BF16)


def _head_row(t_scr, g, e, c0, n):
    return t_scr[pl.ds(HEADS_PER_GROUP * g + e, 1), c0:c0 + n]


def _ssd_stage2_prompt(slot, g, tm, pz_scr, pxbc_scr, hd, carry, state, cw_ref, cb_ref,
                       dvec_ref, nw_ref, wo_ref, xo_ref, cst_ref, hfin_ref, fills, q, j=0):
    half = q // 2
    fills = list(fills)
    acs_scr, eacs_scr, dend_scr, etot_scr, acs_t_scr, dt_t_scr = hd
    if fills:
        fills.pop(0)()
    cw = cw_ref[0, g]
    pxbc_scr[slot, 0:HALO, :] = carry[g]
    acc = cb_ref[0, g] + pxbc_scr[slot, pl.ds(HALO - 3, tm), :] * cw[0:1]
    for k in (1, 2, 3):
        acc = acc + pxbc_scr[slot, pl.ds(HALO - 3 + k, tm), :] * cw[k:k + 1]
    last = pxbc_scr[slot, pl.ds(tm, HALO), :]
    carry[g] = last
    cst_ref[0, j] = last
    xbc = _silu(acc)
    xs = xbc[:, :GROUP_W]
    xsb = xs.astype(BF16)
    bb = xbc[:, GROUP_W:GROUP_W + D_STATE].astype(BF16)
    cc = xbc[:, GROUP_W + D_STATE:].astype(BF16)
    ii = lax.broadcasted_iota(jnp.int32, (half, half), 0)
    jj = lax.broadcasted_iota(jnp.int32, (half, half), 1)
    tri = ii >= jj
    lane_head = lax.broadcasted_iota(jnp.int32, (q, GROUP_W), 1) >> 6
    hst = state[g]
    ys = []
    for c in range(tm // q):
        if fills:
            fills.pop(0)()
        r0 = c * q
        sl = slice(r0, r0 + q)
        acs = _group_lanes(acs_scr[sl], g)
        scale = _head_cols(_group_lanes(eacs_scr[sl], g), GROUP_W)
        dend = _head_cols(_group_lanes(dend_scr[sl], g), GROUP_W)
        etot = _group_lanes(etot_scr[r0:r0 + HALO], g)[0:1]
        cbm = _dot_nt(cc[sl], bb[sl])
        top, bot = None, None
        for e in range(HEADS_PER_GROUP):
            arow = _head_row(acs_t_scr, g, e, r0, q)
            drow = _head_row(dt_t_scr, g, e, r0, q)
            xe = jnp.where(lane_head == e, xsb[sl], jnp.zeros((q, GROUP_W), BF16))
            m00 = _intra(cbm[:half, :half], acs[:half], e, arow[:, :half], drow[:, :half], tri)
            m10 = _intra(cbm[half:, :half], acs[half:], e, arow[:, :half], drow[:, :half], None)
            m11 = _intra(cbm[half:, half:], acs[half:], e, arow[:, half:], drow[:, half:], tri)
            d0 = _dot(m00, xe[:half])
            d1 = _dot(jnp.concatenate([m10, m11], axis=1), xe)
            top = d0 if top is None else top + d0
            bot = d1 if bot is None else bot + d1
        ydiag = jnp.concatenate([top, bot], axis=0)
        yoff = _dot_nt(cc[sl], hst.astype(BF16)) * scale
        ys.append(ydiag + yoff)
        s_new = _dot_tn((xs[sl] * dend).astype(BF16), bb[sl])
        hst = hst * _head_rows(etot) + s_new
    for fill in fills:
        fill()
    state[g] = hst
    hfin_ref[0, j * HEADS_PER_GROUP:(j + 1) * HEADS_PER_GROUP] = hst.reshape(HEADS_PER_GROUP, SSD_HEAD_DIM, D_STATE)
    y = ys[0] if len(ys) == 1 else jnp.concatenate(ys, axis=0)
    _ssd_finish(g, y, xs, pz_scr[slot], dvec_ref, nw_ref, wo_ref, xo_ref, j)


def _ssd_prompt_kernel(h_ref, xg_ref, wz_ref, wx_ref, wb_ref, wc_ref, wdt_ref, dtb_ref, alog_ref, cw_ref, cb_ref,
                       dvec_ref, nw_ref, wo_ref, xo_ref, cst_ref, hfin_ref, pz_scr, pxbc_scr, *scr, q):
    hd, (carry, state) = scr[:6], scr[6:]
    t = pl.program_id(1)
    s = pl.program_id(2)
    tm = h_ref.shape[0]
    rows = pl.ds(HALO, tm)

    @pl.when((s == 0) & (t == 0))
    def _():
        carry[...] = jnp.zeros(carry.shape, F32)
        state[...] = jnp.zeros(state.shape, F32)

    @pl.when(s == 0)
    def _():
        xo_ref[...] = xg_ref[...]
        _ssd_heads(h_ref, wdt_ref, dtb_ref, alog_ref, q, hd)

    def stage1(j):
        def proj_z():
            pz_scr[j] = _dot(h_ref[...], wz_ref[0, :, j * GROUP_W:(j + 1) * GROUP_W])

        def proj_x():
            pxbc_scr[j, rows, 0:GROUP_W] = _dot(h_ref[...], wx_ref[0, :, j * GROUP_W:(j + 1) * GROUP_W])

        def proj_bc():
            w_bc = jnp.concatenate([wb_ref[0, :, j * D_STATE:(j + 1) * D_STATE],
                                    wc_ref[0, :, j * D_STATE:(j + 1) * D_STATE]], axis=1)
            pxbc_scr[j, rows, GROUP_W:XBC_G] = _dot(h_ref[...], w_bc)

        return [proj_z, proj_x, proj_bc]

    def stage2(j, fills):
        _ssd_stage2_prompt(j, PAIR * s + j, tm, pz_scr, pxbc_scr, hd, carry, state, cw_ref, cb_ref,
                           dvec_ref, nw_ref, wo_ref, xo_ref, cst_ref, hfin_ref, fills, q, j)

    for piece in stage1(0):
        piece()
    for j in range(PAIR):
        stage2(j, stage1(j + 1) if j + 1 < PAIR else [])


def _ssd_prompt_call(l, h, xg, w, *, batch, tm):
    m = h.shape[0]
    nt = m // (batch * tm)
    row = lambda b, t, s: (b * nt + t, 0)
    col = lambda width, start: pl.BlockSpec((1, D_MODEL, PAIR * width),
                                            lambda b, t, s: (l, 0, start // (PAIR * width) + s))
    whole = lambda r, width: pl.BlockSpec((1, SSD_GROUPS, r, width), lambda b, t, s: (l, 0, 0, 0))
    in_specs = [
        pl.BlockSpec((tm, D_MODEL), row), pl.BlockSpec((tm, D_MODEL), row),
        col(GROUP_W, 0), col(GROUP_W, S_X), col(D_STATE, S_B), col(D_STATE, S_C),
        pl.BlockSpec((1, D_MODEL, LANES), lambda b, t, s: (l, 0, S_DT // LANES)),
        pl.BlockSpec((1, 1, LANES), lambda b, t, s: (l, 0, 0)),
        pl.BlockSpec((1, 1, LANES), lambda b, t, s: (l, 0, 0)),
        whole(SSD_CONV, XBC_G), whole(1, XBC_G), whole(1, GROUP_W), whole(1, GROUP_W),
        pl.BlockSpec((1, PAIR, GROUP_W, D_MODEL), lambda b, t, s: (l, s, 0, 0)),
    ]
    return pl.pallas_call(
        functools.partial(_ssd_prompt_kernel, q=SSD_Q),
        out_shape=(jax.ShapeDtypeStruct((m, D_MODEL), F32),
                   jax.ShapeDtypeStruct((batch * nt, SSD_GROUPS, HALO, XBC_G), F32),
                   jax.ShapeDtypeStruct((batch * nt, SSD_HEADS, SSD_HEAD_DIM, D_STATE), F32)),
        grid=(batch, nt, SSD_GROUPS // PAIR),
        in_specs=in_specs,
        out_specs=(
            pl.BlockSpec((tm, D_MODEL), row),
            pl.BlockSpec((1, PAIR, HALO, XBC_G), lambda b, t, s: (b * nt + t, s, 0, 0)),
            pl.BlockSpec((1, PAIR * HEADS_PER_GROUP, SSD_HEAD_DIM, D_STATE), lambda b, t, s: (b * nt + t, s, 0, 0)),
        ),
        scratch_shapes=_ssd_scratch(tm, PAIR)
        + [pltpu.VMEM((SSD_GROUPS, HALO, XBC_G), F32), pltpu.VMEM((SSD_GROUPS, GROUP_W, D_STATE), F32)],
        compiler_params=pltpu.CompilerParams(
            dimension_semantics=("arbitrary", "arbitrary", "arbitrary"), vmem_limit_bytes=VMEM_LIMIT),
        name="ssd_prompt_group",
    )(h, xg, *_ssd_weight_args(w))


def _ssd_sample_kernel(h_ref, xg_ref, wz_ref, wx_ref, wb_ref, wc_ref, wdt_ref, dtb_ref, alog_ref, cw_ref, cb_ref, dvec_ref, nw_ref, wo_ref,
                       halo_ref, h0_ref, xo_ref, cst_ref, hfin_ref, pz_scr, pxbc_scr, *hd, n_seq):
    acs_scr, eacs_scr, dend_scr, etot_scr, acs_t_scr, dt_t_scr = hd
    g = pl.program_id(1)
    tm = h_ref.shape[0]
    lt = tm // n_seq
    lt_shift = lt.bit_length() - 1

    @pl.when(g == 0)
    def _():
        xo_ref[...] = xg_ref[...]
        _ssd_heads(h_ref, wdt_ref, dtb_ref, alog_ref, lt, hd)

    for piece in _ssd_stage1(h_ref, (wz_ref, wx_ref, wb_ref, wc_ref), pz_scr, pxbc_scr, 0):
        piece()
    cw = cw_ref[0, g]
    pre3 = pxbc_scr[0, pl.ds(HALO, tm), :].reshape(n_seq, lt, XBC_G)
    xp = jnp.concatenate([halo_ref[0], pre3], axis=1)
    acc = cb_ref[0, g] + xp[:, HALO - 3:HALO - 3 + lt] * cw[0:1]
    for k in (1, 2, 3):
        acc = acc + xp[:, HALO - 3 + k:HALO - 3 + k + lt] * cw[k:k + 1]
    cst_ref[0] = pre3[:, lt - HALO:]
    xbc = _silu(acc.reshape(tm, XBC_G))
    xs = xbc[:, :GROUP_W]
    xsb = xs.astype(BF16)
    bb = xbc[:, GROUP_W:GROUP_W + D_STATE].astype(BF16)
    cc = xbc[:, GROUP_W + D_STATE:].astype(BF16)
    ii = lax.broadcasted_iota(jnp.int32, (tm, tm), 0)
    jj = lax.broadcasted_iota(jnp.int32, (tm, tm), 1)
    mask = ((ii >> lt_shift) == (jj >> lt_shift)) & (ii >= jj)
    acs = _group_lanes(acs_scr[...], g)
    scale = _head_cols(_group_lanes(eacs_scr[...], g), GROUP_W)
    dend = _head_cols(_group_lanes(dend_scr[...], g), GROUP_W)
    cd = _group_lanes(etot_scr[...], g)
    cbm = _dot_nt(cc, bb)
    lane_head = lax.broadcasted_iota(jnp.int32, (tm, GROUP_W), 1) >> 6
    ydiag = None
    for e in range(HEADS_PER_GROUP):
        mp = _intra(cbm, acs, e, _head_row(acs_t_scr, g, e, 0, tm), _head_row(dt_t_scr, g, e, 0, tm), mask)
        d = _dot(mp, jnp.where(lane_head == e, xsb, jnp.zeros((tm, GROUP_W), BF16)))
        ydiag = d if ydiag is None else ydiag + d
    xdtd = (xs * dend).astype(BF16)
    row_seq = lax.broadcasted_iota(jnp.int32, (tm, D_STATE), 0) >> lt_shift
    zero = jnp.zeros((tm, D_STATE), BF16)
    h0 = [h0_ref[0, i].reshape(GROUP_W, D_STATE) for i in range(n_seq)]
    h0_wide = jnp.concatenate(h0, axis=1).astype(BF16)
    c_wide = jnp.concatenate([jnp.where(row_seq == i, cc, zero) for i in range(n_seq)], axis=1)
    b_wide = jnp.concatenate([jnp.where(row_seq == i, bb, zero) for i in range(n_seq)], axis=1)
    yoff = _dot_nt(c_wide, h0_wide) * scale
    s_new = _dot_tn(xdtd, b_wide)
    for i in range(n_seq):
        hn = h0[i] * _head_rows(cd[i * lt:i * lt + 1, :]) + s_new[:, i * D_STATE:(i + 1) * D_STATE]
        hfin_ref[i] = hn.reshape(HEADS_PER_GROUP, SSD_HEAD_DIM, D_STATE)
    _ssd_finish(g, ydiag + yoff, xs, pz_scr[0], dvec_ref, nw_ref, wo_ref, xo_ref)


def _ssd_weight_specs(l, g1, g2):
    whole = lambda rows, width: pl.BlockSpec((1, SSD_GROUPS, rows, width), lambda *a: (l, 0, 0, 0))
    col = lambda width, start: pl.BlockSpec((1, D_MODEL, width), lambda *a: (l, 0, start // width + g1(*a)))
    return [
        col(GROUP_W, 0), col(GROUP_W, S_X), col(D_STATE, S_B), col(D_STATE, S_C),
        pl.BlockSpec((1, D_MODEL, LANES), lambda *a: (l, 0, S_DT // LANES)),
        pl.BlockSpec((1, 1, LANES), lambda *a: (l, 0, 0)),
        pl.BlockSpec((1, 1, LANES), lambda *a: (l, 0, 0)),
        whole(SSD_CONV, XBC_G), whole(1, XBC_G), whole(1, GROUP_W), whole(1, GROUP_W),
        pl.BlockSpec((1, 1, GROUP_W, D_MODEL), lambda *a: (l, g2(*a), 0, 0)),
    ]


def _ssd_weight_args(w):
    return [w["w_in"]] * 5 + [w["dtb"], w["alog"], w["ssd_cw"], w["ssd_cb"], w["dvec"], w["ssd_nw"], w["wo"]]


def _ssd_scratch(tm, slots):
    return ([pltpu.VMEM((slots, tm, GROUP_W), F32), pltpu.VMEM((slots, HALO + tm, XBC_G), F32)]
            + [pltpu.VMEM((tm, LANES), F32)] * 4 + [pltpu.VMEM((LANES, tm), F32)] * 2)


def _ssd_sample_call(l, h, xg, w, halo, h0, *, n_seq, lt):
    m = h.shape[0]
    tm = n_seq * lt
    nt = m // tm
    n_all = m // lt
    kern = functools.partial(_ssd_sample_kernel, n_seq=n_seq)
    row = lambda i, g: (i, 0)
    gi = lambda i, g: g
    return pl.pallas_call(
        kern,
        out_shape=(jax.ShapeDtypeStruct((m, D_MODEL), F32),
                   jax.ShapeDtypeStruct((SSD_GROUPS, n_all, HALO, XBC_G), F32),
                   jax.ShapeDtypeStruct((n_all, SSD_HEADS, SSD_HEAD_DIM, D_STATE), F32)),
        grid=(nt, SSD_GROUPS),
        in_specs=[pl.BlockSpec((tm, D_MODEL), row), pl.BlockSpec((tm, D_MODEL), row)]
        + _ssd_weight_specs(l, gi, gi)
        + [pl.BlockSpec((1, n_seq, HALO, XBC_G), lambda i, g: (g, i, 0, 0)),
           pl.BlockSpec((1, n_seq, HEADS_PER_GROUP, SSD_HEAD_DIM, D_STATE), lambda i, g: (l, i, g, 0, 0))],
        out_specs=(
            pl.BlockSpec((tm, D_MODEL), row),
            pl.BlockSpec((1, n_seq, HALO, XBC_G), lambda i, g: (g, i, 0, 0)),
            pl.BlockSpec((n_seq, HEADS_PER_GROUP, SSD_HEAD_DIM, D_STATE), lambda i, g: (i, g, 0, 0)),
        ),
        scratch_shapes=_ssd_scratch(tm, 1),
        compiler_params=pltpu.CompilerParams(
            dimension_semantics=("arbitrary", "arbitrary"), vmem_limit_bytes=VMEM_LIMIT),
        name="ssd_sample_group",
    )(h, xg, *_ssd_weight_args(w), halo, h0)


def _ffn_kernel(*refs, n_seq, carried, final):
    if carried:
        (x_ref, n2_ref, wg_ref, wv_ref, cw_ref, cb_ref, wd_ref, fn_ref,
         xo_ref, lg_ref, lv_ref, h_scr, carry_g, carry_v) = refs
    else:
        (x_ref, n2_ref, wg_ref, wv_ref, cw_ref, cb_ref, wd_ref, fn_ref, hg_ref, hv_ref,
         xo_ref, lg_ref, lv_ref, h_scr) = refs
    nax = 3 if carried else 2
    f = pl.program_id(nax - 1)
    tm = x_ref.shape[0]
    lt = tm // n_seq
    tf = TF

    @pl.when(f == 0)
    def _():
        x = x_ref[...]
        h_scr[...] = _rms(x, n2_ref[0]).astype(BF16)
        xo_ref[...] = x

    def conv(u, halo, cw, cb):
        if n_seq == 1:
            xp = jnp.concatenate([halo[0], u], axis=0)
            acc = cb + xp[HALO - 2:HALO - 2 + tm] * cw[0:1]
            acc = acc + xp[HALO - 1:HALO - 1 + tm] * cw[1:2]
            acc = acc + xp[HALO:] * cw[2:3]
            return acc, u[tm - HALO:].reshape(1, HALO, tf)
        u3 = u.reshape(n_seq, lt, tf)
        xp = jnp.concatenate([halo, u3], axis=1)
        acc = cb + xp[:, HALO - 2:HALO - 2 + lt] * cw[0:1]
        acc = acc + xp[:, HALO - 1:HALO - 1 + lt] * cw[1:2]
        acc = acc + xp[:, HALO:] * cw[2:3]
        return acc.reshape(tm, tf), u3[:, lt - HALO:]

    if carried:
        @pl.when((f == 0) & (pl.program_id(1) == 0))
        def _():
            carry_g[...] = jnp.zeros(carry_g.shape, F32)
            carry_v[...] = jnp.zeros(carry_v.shape, F32)
        halo_g = carry_g[f].reshape(1, HALO, tf)
        halo_v = carry_v[f].reshape(1, HALO, tf)
    else:
        halo_g = hg_ref[...]
        halo_v = hv_ref[...]
    hb = h_scr[...]
    cw = cw_ref[0, f]
    cb = cb_ref[0, f]
    cg, last_g = conv(_dot(hb, wg_ref[0]), halo_g, cw[:, :tf], cb[:, :tf])
    cv, last_v = conv(_dot(hb, wv_ref[0]), halo_v, cw[:, tf:], cb[:, tf:])
    lg_ref[...] = last_g
    lv_ref[...] = last_v
    if carried:
        carry_g[f] = last_g[0]
        carry_v[f] = last_v[0]
    act = (_silu(cg) * cv).astype(BF16)
    xo_ref[...] += _dot(act, wd_ref[0])

    if final:
        @pl.when(f == NF - 1)
        def _():
            xo_ref[...] = _rms(xo_ref[...], fn_ref[...])


def _ffn_call(l, x, w, fnw, halos, *, batch, tm, n_seq, final):
    m = x.shape[0]
    carried = halos is None
    kern = functools.partial(_ffn_kernel, n_seq=n_seq, carried=carried, final=final)
    if carried:
        nt = m // (batch * tm)
        grid = (batch, nt, NF)
        row = lambda b, t, f: (b * nt + t, 0)
        fi = lambda b, t, f: f
        si = lambda b, t, f: b * nt + t
        n_state = batch * nt
    else:
        grid = (m // tm, NF)
        row = lambda i, f: (i, 0)
        fi = lambda i, f: f
        si = lambda i, f: i
        n_state = m // (tm // n_seq)
    in_specs = [
        _x_tile_spec(tm, row, tm > TM),
        pl.BlockSpec((1, 1, D_MODEL), lambda *a: (l, 0, 0)),
        pl.BlockSpec((1, D_MODEL, TF), lambda *a: (l, 0, fi(*a))),
        pl.BlockSpec((1, D_MODEL, TF), lambda *a: (l, 0, NF + fi(*a))),
        pl.BlockSpec((1, NF, FFN_CONV, 2 * TF), lambda *a: (l, 0, 0, 0)),
        pl.BlockSpec((1, NF, 1, 2 * TF), lambda *a: (l, 0, 0, 0)),
        pl.BlockSpec((1, TF, D_MODEL), lambda *a: (l, fi(*a), 0)),
        pl.BlockSpec((1, D_MODEL), lambda *a: (0, 0)),
    ]
    args = [x, w["n2"], w["w_up"], w["w_up"], w["ffn_cw"], w["ffn_cb"], w["w_down"], fnw]
    scratch = [pltpu.VMEM((tm, D_MODEL), BF16)]
    if carried:
        scratch += [pltpu.VMEM((NF, HALO, TF), F32), pltpu.VMEM((NF, HALO, TF), F32)]
    else:
        in_specs += [pl.BlockSpec((n_seq, HALO, TF), lambda *a: (si(*a), 0, fi(*a))),
                     pl.BlockSpec((n_seq, HALO, TF), lambda *a: (si(*a), 0, fi(*a) + NF))]
        args += [halos, halos]
    st_rows = 1 if carried else n_seq
    return pl.pallas_call(
        kern,
        out_shape=(jax.ShapeDtypeStruct((m, D_MODEL), F32),
                   jax.ShapeDtypeStruct((n_state, HALO, D_FF), F32),
                   jax.ShapeDtypeStruct((n_state, HALO, D_FF), F32)),
        grid=grid,
        in_specs=in_specs,
        out_specs=(
            pl.BlockSpec((tm, D_MODEL), row),
            pl.BlockSpec((st_rows, HALO, TF), lambda *a: (si(*a), 0, fi(*a))),
            pl.BlockSpec((st_rows, HALO, TF), lambda *a: (si(*a), 0, fi(*a))),
        ),
        scratch_shapes=scratch,
        compiler_params=pltpu.CompilerParams(
            dimension_semantics=("arbitrary",) * len(grid), vmem_limit_bytes=VMEM_LIMIT),
        name="ffn_prompt" if carried else "ffn_sample",
    )(*args)


def _group_cols(w, n_groups):
    return w.reshape(w.shape[:-1] + (n_groups, w.shape[-1] // n_groups))


def _xbc_to_groups(a):
    xs = _group_cols(a[..., :D_SSD], SSD_GROUPS)
    bs = _group_cols(a[..., D_SSD:D_SSD + SSD_GROUPS * D_STATE], SSD_GROUPS)
    cs = _group_cols(a[..., D_SSD + SSD_GROUPS * D_STATE:], SSD_GROUPS)
    return jnp.concatenate([xs, bs, cs], axis=-1)


def _groups_to_xbc(a):
    lead = a.shape[:-2]
    xs = a[..., :GROUP_W].reshape(lead + (D_SSD,))
    bs = a[..., GROUP_W:GROUP_W + D_STATE].reshape(lead + (SSD_GROUPS * D_STATE,))
    cs = a[..., GROUP_W + D_STATE:].reshape(lead + (SSD_GROUPS * D_STATE,))
    return jnp.concatenate([xs, bs, cs], axis=-1)


def _ffn_blocks(a, nf):
    return jnp.concatenate([_group_cols(a[..., :D_FF], nf), _group_cols(a[..., D_FF:], nf)], axis=-1)


def _pad_halo(state):
    return jnp.pad(state, ((0, 0), (HALO - state.shape[1], 0), (0, 0)))


def _weights(norm1_w, w_in, ssd_conv_w, ssd_conv_b, dt_bias, a_log, ssd_d, ssd_norm_w, gmlp_norm_w, w_out, norm2_w,
             w_up, ffn_conv_w, ffn_conv_b, w_down):
    d = w_in.shape[0]
    return dict(
        n1=norm1_w[:, None, :], n2=norm2_w[:, None, :],
        w_in=w_in.astype(BF16),
        w_uv=w_in[:, :, S_U:].astype(BF16),
        wo=w_out.astype(BF16).reshape(d, 2 * SSD_GROUPS, GROUP_W, D_MODEL),
        w_up=w_up.astype(BF16),
        w_down=w_down.astype(BF16),
        dtb=jnp.pad(dt_bias, ((0, 0), (0, LANES - SSD_HEADS)))[:, None, :],
        alog=jnp.pad(a_log, ((0, 0), (0, LANES - SSD_HEADS)))[:, None, :],
        ssd_cw=_xbc_to_groups(ssd_conv_w).transpose(0, 2, 1, 3),
        ssd_cb=_xbc_to_groups(ssd_conv_b)[:, :, None, :],
        dvec=jnp.repeat(ssd_d, SSD_HEAD_DIM, axis=-1).reshape(d, SSD_GROUPS, 1, GROUP_W),
        ssd_nw=ssd_norm_w.reshape(d, SSD_GROUPS, 1, GROUP_W),
        gnw=gmlp_norm_w.reshape(d, GMLP_GROUPS, 1, GROUP_W),
        ffn_cw=_ffn_blocks(ffn_conv_w, NF).transpose(0, 2, 1, 3),
        ffn_cb=_ffn_blocks(ffn_conv_b, NF)[:, :, None, :],
    )


def kernel(x_prompt, x_sample, state_ssd_conv, state_ssd, state_ffn_conv, norm1_w, w_in, ssd_conv_w, ssd_conv_b,
           dt_bias, a_log, ssd_d, ssd_norm_w, gmlp_norm_w, gmlp_w_s, gmlp_b_s, w_out, norm2_w, w_up, ffn_conv_w,
           ffn_conv_b, w_down, final_norm_w):
    bp, lp, _ = x_prompt.shape
    bs, ls, _ = x_sample.shape
    xp = x_prompt.reshape(bp * lp, D_MODEL)
    xs = x_sample.reshape(bs * ls, D_MODEL)
    ms = bs * ls
    fnw = final_norm_w[None, :]
    ssd_seqs = 16
    ntp = lp // TM
    w = _weights(norm1_w, w_in, ssd_conv_w, ssd_conv_b, dt_bias, a_log, ssd_d, ssd_norm_w, gmlp_norm_w, w_out,
                 norm2_w, w_up, ffn_conv_w, ffn_conv_b, w_down)
    ws_p = gmlp_w_s
    bs_p = gmlp_b_s[:, :, :, None]
    ws_s = jnp.tile(gmlp_w_s[:, :, :ls, :ls], (1, 1, 1, bs))
    bs_s = jnp.tile(gmlp_b_s[:, :, :ls], (1, 1, bs))[:, :, :, None]
    p_conv, p_ssd, p_ffn = [], [], []
    s_conv, s_ssd, s_ffn, s_v = [], [], [], []
    for l in range(DEPTH):
        final = l == DEPTH - 1
        xg, h = _gmlp_tile_call(l, xp, w, ws_p, bs_p, tm=TM)
        x1, cst, hfin = _ssd_prompt_call(l, h, xg, w, batch=bp, tm=TM)
        xp, lg, lv = _ffn_call(l, x1, w, fnw, None, batch=bp, tm=TM_WIDE, n_seq=1, final=final)
        cst, hfin = (a.reshape((bp, ntp) + a.shape[1:])[:, -1] for a in (cst, hfin))
        lg, lv = (a.reshape((bp, lp // TM_WIDE) + a.shape[1:])[:, -1] for a in (lg, lv))
        p_conv.append(_groups_to_xbc(cst[:, :, HALO - (SSD_CONV - 1):, :].transpose(0, 2, 1, 3)))
        p_ssd.append(hfin)
        p_ffn.append(jnp.concatenate([lg, lv], axis=-1)[:, HALO - (FFN_CONV - 1):, :])
        xg, h, vn = _gmlp_call(l, xs, w, ws_s, bs_s, tm=ms, chunk=ms, seg_len=ls, emit_vn=True)
        halo = _pad_halo(_xbc_to_groups(state_ssd_conv[l]).transpose(2, 0, 1, 3).reshape(
            SSD_GROUPS * bs, SSD_CONV - 1, XBC_G)).reshape(SSD_GROUPS, bs, HALO, XBC_G)
        x1, cst, hfin = _ssd_sample_call(l, h, xg, w, halo, state_ssd, n_seq=ssd_seqs, lt=ls)
        xs, lg, lv = _ffn_call(l, x1, w, fnw, _pad_halo(state_ffn_conv[l]), batch=1, tm=ms, n_seq=bs, final=final)
        s_conv.append(_groups_to_xbc(cst[:, :, HALO - (SSD_CONV - 1):, :].transpose(1, 2, 0, 3)))
        s_ssd.append(hfin)
        s_ffn.append(jnp.concatenate([lg, lv], axis=-1)[:, HALO - (FFN_CONV - 1):, :])
        s_v.append(vn.reshape(bs, ls, D_GMLP))
    return (xp.reshape(bp, lp, D_MODEL), xs.reshape(bs, ls, D_MODEL),
            jnp.stack(p_conv), jnp.stack(p_ssd), jnp.stack(p_ffn),
            jnp.stack(s_conv), jnp.stack(s_ssd), jnp.stack(s_ffn), jnp.stack(s_v))
```

```python
import functools

import jax
import jax.numpy as jnp
from jax import lax
from jax.experimental import pallas as pl
from jax.experimental.pallas import tpu as pltpu

D_MODEL = 2048
DEPTH = 2
CHUNK = 64
D_SSD = 2048
SSD_HEAD_DIM = 64
SSD_HEADS = 32
SSD_GROUPS = 8
HEADS_PER_GROUP = 4
D_STATE = 128
SSD_CONV = 4
D_XBC = 4096
D_GMLP = 2048
GMLP_GROUPS = 8
GMLP_CHUNK = 128
D_FF = 5632
FFN_CONV = 3
EPS = 1e-6

LANES = 128
GROUP_W = 256
XBC_G = GROUP_W + 2 * D_STATE
SSD_COLS = GROUP_W + XBC_G
HALO = 8
TM = 512
TM_WIDE = 1024
SSD_Q = 256
PAIR = 2
TF = 512
NF = D_FF // TF
VMEM_LIMIT = 60 * 1024 * 1024

S_X = D_SSD
S_B = S_X + D_SSD
S_C = S_B + SSD_GROUPS * D_STATE
S_DT = S_C + SSD_GROUPS * D_STATE
S_U = S_DT + SSD_HEADS
S_V = S_U + D_GMLP

F32 = jnp.float32
BF16 = jnp.bfloat16


def _dot(a, b):
    return jnp.dot(a, b, preferred_element_type=F32)


def _dot_nt(a, b):
    return lax.dot_general(a, b, (((1,), (1,)), ((), ())), preferred_element_type=F32)


def _dot_tn(a, b):
    return lax.dot_general(a, b, (((0,), (0,)), ((), ())), preferred_element_type=F32)


def _rms(x, w):
    return x * lax.rsqrt(jnp.mean(x * x, axis=-1, keepdims=True) + EPS) * w


def _silu(x):
    return x * jax.nn.sigmoid(x)


def _softplus(x):
    return jnp.maximum(x, 0.0) + jnp.log1p(jnp.exp(-jnp.abs(x)))


def _seg_cumsum(x, seg_len):
    pos = lax.broadcasted_iota(jnp.int32, x.shape, 0) & (seg_len - 1)
    k = 1
    while k < seg_len:
        x = x + jnp.where(pos >= k, pltpu.roll(x, k, 0), 0.0)
        k *= 2
    return x


def _seg_total(x, seg_len):
    n = x.shape[0]
    pos = lax.broadcasted_iota(jnp.int32, x.shape, 0) & (seg_len - 1)
    k = 1
    while k < seg_len:
        x = x + jnp.where((pos & k) == 0, pltpu.roll(x, n - k, 0), pltpu.roll(x, k, 0))
        k *= 2
    return x


def _head_cols(v, width):
    n = v.shape[0]
    lane_head = lax.broadcasted_iota(jnp.int32, (n, width), 1) >> 6
    out = jnp.broadcast_to(v[:, 3:4], (n, width))
    for e in (2, 1, 0):
        out = jnp.where(lane_head == e, jnp.broadcast_to(v[:, e:e + 1], (n, width)), out)
    return out


def _head_rows(row):
    row_head = lax.broadcasted_iota(jnp.int32, (GROUP_W, D_STATE), 0) >> 6
    out = jnp.broadcast_to(row[:, 3:4], (GROUP_W, D_STATE))
    for e in (2, 1, 0):
        out = jnp.where(row_head == e, jnp.broadcast_to(row[:, e:e + 1], (GROUP_W, D_STATE)), out)
    return out


def _gmlp_group(g, h_scr, wu_ref, wv_ref, gnw_ref, ws_ref, bs_ref, wo_ref, xo_ref, vn_ref, *, chunk, seg_len):
    tm = h_scr.shape[0]
    seg_shift = seg_len.bit_length() - 1
    chunk_shift = CHUNK.bit_length() - 1
    val = {}

    def proj_u():
        val["u"] = jax.nn.gelu(_dot(h_scr[...], wu_ref[0]))

    def proj_v():
        vn = _rms(jax.nn.gelu(_dot(h_scr[...], wv_ref[0])), gnw_ref[0, g])
        if vn_ref is not None:
            vn_ref[...] = vn
        val["vnb"] = vn.astype(BF16)

    def gate():
        ii = lax.broadcasted_iota(jnp.int32, (chunk, chunk), 0)
        jj = lax.broadcasted_iota(jnp.int32, (chunk, chunk), 1)
        same_seq = (ii >> seg_shift) == (jj >> seg_shift)
        block_causal = ((jj & (seg_len - 1)) >> chunk_shift) <= ((ii & (seg_len - 1)) >> chunk_shift)
        w_rows = ws_ref[0, 0]
        w_full = w_rows if seg_len == chunk else jnp.concatenate([w_rows] * (chunk // seg_len), axis=0)
        wm = jnp.where(same_seq & block_causal, w_full, 0.0).astype(BF16)
        bias = bs_ref[0, g]
        parts = []
        for c in range(tm // chunk):
            sg = _dot(wm, val["vnb"][c * chunk:(c + 1) * chunk]) + bias
            parts.append(val["u"][c * chunk:(c + 1) * chunk] * sg)
        y = parts[0] if len(parts) == 1 else jnp.concatenate(parts, axis=0)
        xo_ref[...] += _dot(y.astype(BF16), wo_ref[0, 0])

    return [proj_u, proj_v, gate]


def _gmlp_kernel(x_ref, n1_ref, wu_ref, wv_ref, gnw_ref, ws_ref, bs_ref, wo_ref, xo_ref, h_ref, *rest,
                 chunk, seg_len, emit_vn):
    vn_ref = rest[0] if emit_vn else None
    h_scr = rest[-1]
    g = pl.program_id(1)

    @pl.when(g == 0)
    def _():
        x = x_ref[...]
        hb = _rms(x, n1_ref[0]).astype(BF16)
        h_scr[...] = hb
        h_ref[...] = hb
        xo_ref[...] = x

    for piece in _gmlp_group(g, h_scr, wu_ref, wv_ref, gnw_ref, ws_ref, bs_ref, wo_ref, xo_ref, vn_ref,
                             chunk=chunk, seg_len=seg_len):
        piece()


def _gmlp_tile_kernel(x_ref, n1_ref, w_ref, gnw_ref, ws_ref, bs_ref, wo_ref, xo_ref, h_ref, h_scr):
    tm = x_ref.shape[0]
    chunk = GMLP_CHUNK
    chunk_shift = CHUNK.bit_length() - 1
    x = x_ref[...]
    hb = _rms(x, n1_ref[0]).astype(BF16)
    h_scr[...] = hb
    h_ref[...] = hb
    xo_ref[...] = x
    ii = lax.broadcasted_iota(jnp.int32, (chunk, chunk), 0)
    jj = lax.broadcasted_iota(jnp.int32, (chunk, chunk), 1)
    block_causal = (jj >> chunk_shift) <= (ii >> chunk_shift)

    def project(g):
        u = jax.nn.gelu(_dot(h_scr[...], w_ref[0, :, g * GROUP_W:(g + 1) * GROUP_W]))
        v = jax.nn.gelu(_dot(h_scr[...], w_ref[0, :, D_GMLP + g * GROUP_W:D_GMLP + (g + 1) * GROUP_W]))
        return u, v

    def gate(g, u, v):
        vnb = _rms(v, gnw_ref[0, g]).astype(BF16)
        wm = jnp.where(block_causal, ws_ref[0, g], 0.0).astype(BF16)
        bias = bs_ref[0, g]
        parts = []
        for c in range(tm // chunk):
            sg = _dot(wm, vnb[c * chunk:(c + 1) * chunk]) + bias
            parts.append(u[c * chunk:(c + 1) * chunk] * sg)
        y = jnp.concatenate(parts, axis=0)
        xo_ref[...] += _dot(y.astype(BF16), wo_ref[0, g])

    uv = project(0)
    for g in range(GMLP_GROUPS):
        nxt = project(g + 1) if g + 1 < GMLP_GROUPS else None
        gate(g, *uv)
        uv = nxt


def _gmlp_tile_call(l, x, w, ws, bs, *, tm):
    m = x.shape[0]
    once = dict(pipeline_mode=pl.Buffered(1))
    return pl.pallas_call(
        _gmlp_tile_kernel,
        out_shape=(jax.ShapeDtypeStruct((m, D_MODEL), F32), jax.ShapeDtypeStruct((m, D_MODEL), BF16)),
        grid=(m // tm,),
        in_specs=[
            pl.BlockSpec((tm, D_MODEL), lambda i: (i, 0)),
            pl.BlockSpec((1, 1, D_MODEL), lambda i: (l, 0, 0)),
            pl.BlockSpec((1, D_MODEL, 2 * D_GMLP), lambda i: (l, 0, 0), **once),
            pl.BlockSpec((1, GMLP_GROUPS, 1, GROUP_W), lambda i: (l, 0, 0, 0)),
            pl.BlockSpec((1, GMLP_GROUPS, GMLP_CHUNK, GMLP_CHUNK), lambda i: (l, 0, 0, 0), **once),
            pl.BlockSpec((1, GMLP_GROUPS, GMLP_CHUNK, 1), lambda i: (l, 0, 0, 0), **once),
            pl.BlockSpec((1, GMLP_GROUPS, GROUP_W, D_MODEL), lambda i: (l, 1, 0, 0), **once),
        ],
        out_specs=(pl.BlockSpec((tm, D_MODEL), lambda i: (i, 0)), pl.BlockSpec((tm, D_MODEL), lambda i: (i, 0))),
        scratch_shapes=[pltpu.VMEM((tm, D_MODEL), BF16)],
        compiler_params=pltpu.CompilerParams(dimension_semantics=("arbitrary",), vmem_limit_bytes=VMEM_LIMIT),
        name="gmlp_tile",
    )(x, w["n1"], w["w_uv"], w["gnw"], ws, bs, w["wo"])


def _x_tile_spec(tm, index_map, single):
    if single:
        return pl.BlockSpec((tm, D_MODEL), index_map, pipeline_mode=pl.Buffered(1))
    return pl.BlockSpec((tm, D_MODEL), index_map)


def _gmlp_call(l, x, w, ws, bs, *, tm, chunk, seg_len, emit_vn):
    m = x.shape[0]
    nt = m // tm
    kern = functools.partial(_gmlp_kernel, chunk=chunk, seg_len=seg_len, emit_vn=emit_vn)
    out_shape = [jax.ShapeDtypeStruct((m, D_MODEL), F32), jax.ShapeDtypeStruct((m, D_MODEL), BF16)]
    out_specs = [pl.BlockSpec((tm, D_MODEL), lambda i, g: (i, 0)), pl.BlockSpec((tm, D_MODEL), lambda i, g: (i, 0))]
    if emit_vn:
        out_shape.append(jax.ShapeDtypeStruct((m, D_GMLP), F32))
        out_specs.append(pl.BlockSpec((tm, GROUP_W), lambda i, g: (i, g)))
    return pl.pallas_call(
        kern,
        out_shape=tuple(out_shape),
        grid=(nt, GMLP_GROUPS),
        in_specs=[
            _x_tile_spec(tm, lambda i, g: (i, 0), tm > TM),
            pl.BlockSpec((1, 1, D_MODEL), lambda i, g: (l, 0, 0)),
            pl.BlockSpec((1, D_MODEL, GROUP_W), lambda i, g: (l, 0, g)),
            pl.BlockSpec((1, D_MODEL, GROUP_W), lambda i, g: (l, 0, GMLP_GROUPS + g)),
            pl.BlockSpec((1, GMLP_GROUPS, 1, GROUP_W), lambda i, g: (l, 0, 0, 0)),
            pl.BlockSpec((1, 1, seg_len, chunk), lambda i, g: (l, g, 0, 0)),
            pl.BlockSpec((1, GMLP_GROUPS, chunk, 1), lambda i, g: (l, 0, 0, 0)),
            pl.BlockSpec((1, 1, GROUP_W, D_MODEL), lambda i, g: (l, SSD_GROUPS + g, 0, 0)),
        ],
        out_specs=tuple(out_specs),
        scratch_shapes=[pltpu.VMEM((tm, D_MODEL), BF16)],
        compiler_params=pltpu.CompilerParams(
            dimension_semantics=("arbitrary", "arbitrary"), vmem_limit_bytes=VMEM_LIMIT),
        name="gmlp_group",
    )(x, w["n1"], w["w_uv"], w["w_uv"], w["gnw"], ws, bs, w["wo"])


def _ssd_heads(h_ref, wdt_ref, dtb_ref, alog_ref, seg_len, hd):
    acs_scr, eacs_scr, dend_scr, etot_scr, acs_t_scr, dt_t_scr = hd
    dt = _softplus(_dot(h_ref[...], wdt_ref[0]) + dtb_ref[0])
    dta = dt * (-jnp.exp(alog_ref[0]))
    acs = _seg_cumsum(dta, seg_len)
    tot = _seg_total(dta, seg_len)
    acs_scr[...] = acs
    eacs_scr[...] = jnp.exp(acs)
    dend_scr[...] = jnp.exp(tot - acs) * dt
    etot_scr[...] = jnp.exp(tot)
    acs_t_scr[...] = acs.T
    dt_t_scr[...] = dt.T


def _group_lanes(v, g):
    return pltpu.roll(v, (LANES - HEADS_PER_GROUP * g) & (LANES - 1), 1)


def _ssd_stage1(h_ref, w_refs, pz_scr, pxbc_scr, slot):
    wz_ref, wx_ref, wb_ref, wc_ref = w_refs
    tm = h_ref.shape[0]
    rows = pl.ds(HALO, tm)

    def proj_z():
        pz_scr[slot] = _dot(h_ref[...], wz_ref[0])

    def proj_x():
        pxbc_scr[slot, rows, 0:GROUP_W] = _dot(h_ref[...], wx_ref[0])

    def proj_bc():
        w_bc = jnp.concatenate([wb_ref[0], wc_ref[0]], axis=1)
        pxbc_scr[slot, rows, GROUP_W:XBC_G] = _dot(h_ref[...], w_bc)

    return [proj_z, proj_x, proj_bc]


def _ssd_finish(g, y, xs, z, dvec_ref, nw_ref, wo_ref, xo_ref, j=0):
    y = y + dvec_ref[0, g] * xs
    y = y * _silu(z)
    y = _rms(y, nw_ref[0, g])
    xo_ref[...] += _dot(y.astype(BF16), wo_ref[0, j])


def _intra(cb, acs, e, acs_row, dt_row, mask):
    seg = acs[:, e:e + 1] - acs_row
    if mask is not None:
        seg = jnp.where(mask, seg, -jnp.inf)
    return (cb * jnp.exp(seg) * dt_row).astype(BF16)


def _head_row(t_scr, g, e, c0, n):
    return t_scr[pl.ds(HEADS_PER_GROUP * g + e, 1), c0:c0 + n]


def _ssd_stage2_prompt(slot, g, tm, pz_scr, pxbc_scr, hd, carry, state, cw_ref, cb_ref,
                       dvec_ref, nw_ref, wo_ref, xo_ref, cst_ref, hfin_ref, q, j):
    half = q // 2
    acs_scr, eacs_scr, dend_scr, etot_scr, acs_t_scr, dt_t_scr = hd
    cw = cw_ref[0, g]
    pxbc_scr[slot, 0:HALO, :] = carry[g]
    acc = cb_ref[0, g] + pxbc_scr[slot, pl.ds(HALO - 3, tm), :] * cw[0:1]
    for k in (1, 2, 3):
        acc = acc + pxbc_scr[slot, pl.ds(HALO - 3 + k, tm), :] * cw[k:k + 1]
    last = pxbc_scr[slot, pl.ds(tm, HALO), :]
    carry[g] = last
    cst_ref[0, j] = last
    xbc = _silu(acc)
    xs = xbc[:, :GROUP_W]
    xsb = xs.astype(BF16)
    bb = xbc[:, GROUP_W:GROUP_W + D_STATE].astype(BF16)
    cc = xbc[:, GROUP_W + D_STATE:].astype(BF16)
    ii = lax.broadcasted_iota(jnp.int32, (half, half), 0)
    jj = lax.broadcasted_iota(jnp.int32, (half, half), 1)
    tri = ii >= jj
    lane_head = lax.broadcasted_iota(jnp.int32, (q, GROUP_W), 1) >> 6
    hst = state[g]
    ys = []
    for c in range(tm // q):
        r0 = c * q
        sl = slice(r0, r0 + q)
        acs = _group_lanes(acs_scr[sl], g)
        scale = _head_cols(_group_lanes(eacs_scr[sl], g), GROUP_W)
        dend = _head_cols(_group_lanes(dend_scr[sl], g), GROUP_W)
        etot = _group_lanes(etot_scr[r0:r0 + HALO], g)[0:1]
        cbm = _dot_nt(cc[sl], bb[sl])
        top, bot = None, None
        for e in range(HEADS_PER_GROUP):
            arow = _head_row(acs_t_scr, g, e, r0, q)
            drow = _head_row(dt_t_scr, g, e, r0, q)
            xe = jnp.where(lane_head == e, xsb[sl], jnp.zeros((q, GROUP_W), BF16))
            m00 = _intra(cbm[:half, :half], acs[:half], e, arow[:, :half], drow[:, :half], tri)
            m10 = _intra(cbm[half:, :half], acs[half:], e, arow[:, :half], drow[:, :half], None)
            m11 = _intra(cbm[half:, half:], acs[half:], e, arow[:, half:], drow[:, half:], tri)
            d0 = _dot(m00, xe[:half])
            d1 = _dot(jnp.concatenate([m10, m11], axis=1), xe)
            top = d0 if top is None else top + d0
            bot = d1 if bot is None else bot + d1
        ydiag = jnp.concatenate([top, bot], axis=0)
        yoff = _dot_nt(cc[sl], hst.astype(BF16)) * scale
        ys.append(ydiag + yoff)
        s_new = _dot_tn((xs[sl] * dend).astype(BF16), bb[sl])
        hst = hst * _head_rows(etot) + s_new
    state[g] = hst
    hfin_ref[0, j * HEADS_PER_GROUP:(j + 1) * HEADS_PER_GROUP] = hst.reshape(HEADS_PER_GROUP, SSD_HEAD_DIM, D_STATE)
    y = ys[0] if len(ys) == 1 else jnp.concatenate(ys, axis=0)
    _ssd_finish(g, y, xs, pz_scr[slot], dvec_ref, nw_ref, wo_ref, xo_ref, j)


def _ssd_prompt_kernel(h_ref, xg_ref, wz_ref, wx_ref, wb_ref, wc_ref, wdt_ref, dtb_ref, alog_ref, cw_ref, cb_ref,
                       dvec_ref, nw_ref, wo_ref, xo_ref, cst_ref, hfin_ref, pz_scr, pxbc_scr, *scr, q):
    hd, (carry, state) = scr[:6], scr[6:]
    t = pl.program_id(1)
    s = pl.program_id(2)
    tm = h_ref.shape[0]
    rows = pl.ds(HALO, tm)

    @pl.when((s == 0) & (t == 0))
    def _():
        carry[...] = jnp.zeros(carry.shape, F32)
        state[...] = jnp.zeros(state.shape, F32)

    @pl.when(s == 0)
    def _():
        xo_ref[...] = xg_ref[...]
        _ssd_heads(h_ref, wdt_ref, dtb_ref, alog_ref, q, hd)

    for j in range(PAIR):
        pz_scr[j] = _dot(h_ref[...], wz_ref[0, :, j * GROUP_W:(j + 1) * GROUP_W])
        pxbc_scr[j, rows, 0:GROUP_W] = _dot(h_ref[...], wx_ref[0, :, j * GROUP_W:(j + 1) * GROUP_W])
        w_bc = jnp.concatenate([wb_ref[0, :, j * D_STATE:(j + 1) * D_STATE],
                                wc_ref[0, :, j * D_STATE:(j + 1) * D_STATE]], axis=1)
        pxbc_scr[j, rows, GROUP_W:XBC_G] = _dot(h_ref[...], w_bc)
    for j in range(PAIR):
        _ssd_stage2_prompt(j, PAIR * s + j, tm, pz_scr, pxbc_scr, hd, carry, state, cw_ref, cb_ref,
                           dvec_ref, nw_ref, wo_ref, xo_ref, cst_ref, hfin_ref, q, j)


def _ssd_prompt_call(l, h, xg, w, *, batch, tm):
    m = h.shape[0]
    nt = m // (batch * tm)
    row = lambda b, t, s: (b * nt + t, 0)
    col = lambda width, start: pl.BlockSpec((1, D_MODEL, PAIR * width),
                                            lambda b, t, s: (l, 0, start // (PAIR * width) + s))
    whole = lambda r, width: pl.BlockSpec((1, SSD_GROUPS, r, width), lambda b, t, s: (l, 0, 0, 0))
    in_specs = [
        pl.BlockSpec((tm, D_MODEL), row), pl.BlockSpec((tm, D_MODEL), row),
        col(GROUP_W, 0), col(GROUP_W, S_X), col(D_STATE, S_B), col(D_STATE, S_C),
        pl.BlockSpec((1, D_MODEL, LANES), lambda b, t, s: (l, 0, S_DT // LANES)),
        pl.BlockSpec((1, 1, LANES), lambda b, t, s: (l, 0, 0)),
        pl.BlockSpec((1, 1, LANES), lambda b, t, s: (l, 0, 0)),
        whole(SSD_CONV, XBC_G), whole(1, XBC_G), whole(1, GROUP_W), whole(1, GROUP_W),
        pl.BlockSpec((1, PAIR, GROUP_W, D_MODEL), lambda b, t, s: (l, s, 0, 0)),
    ]
    return pl.pallas_call(
        functools.partial(_ssd_prompt_kernel, q=SSD_Q),
        out_shape=(jax.ShapeDtypeStruct((m, D_MODEL), F32),
                   jax.ShapeDtypeStruct((batch * nt, SSD_GROUPS, HALO, XBC_G), F32),
                   jax.ShapeDtypeStruct((batch * nt, SSD_HEADS, SSD_HEAD_DIM, D_STATE), F32)),
        grid=(batch, nt, SSD_GROUPS // PAIR),
        in_specs=in_specs,
        out_specs=(
            pl.BlockSpec((tm, D_MODEL), row),
            pl.BlockSpec((1, PAIR, HALO, XBC_G), lambda b, t, s: (b * nt + t, s, 0, 0)),
            pl.BlockSpec((1, PAIR * HEADS_PER_GROUP, SSD_HEAD_DIM, D_STATE), lambda b, t, s: (b * nt + t, s, 0, 0)),
        ),
        scratch_shapes=_ssd_scratch(tm, PAIR)
        + [pltpu.VMEM((SSD_GROUPS, HALO, XBC_G), F32), pltpu.VMEM((SSD_GROUPS, GROUP_W, D_STATE), F32)],
        compiler_params=pltpu.CompilerParams(
            dimension_semantics=("arbitrary", "arbitrary", "arbitrary"), vmem_limit_bytes=VMEM_LIMIT),
        name="ssd_prompt_group",
    )(h, xg, *_ssd_weight_args(w))


def _ssd_sample_kernel(h_ref, xg_ref, wz_ref, wx_ref, wb_ref, wc_ref, wdt_ref, dtb_ref, alog_ref, cw_ref, cb_ref, dvec_ref, nw_ref, wo_ref,
                       halo_ref, h0_ref, xo_ref, cst_ref, hfin_ref, pz_scr, pxbc_scr, *hd, n_seq):
    acs_scr, eacs_scr, dend_scr, etot_scr, acs_t_scr, dt_t_scr = hd
    g = pl.program_id(1)
    tm = h_ref.shape[0]
    lt = tm // n_seq
    lt_shift = lt.bit_length() - 1

    @pl.when(g == 0)
    def _():
        xo_ref[...] = xg_ref[...]
        _ssd_heads(h_ref, wdt_ref, dtb_ref, alog_ref, lt, hd)

    for piece in _ssd_stage1(h_ref, (wz_ref, wx_ref, wb_ref, wc_ref), pz_scr, pxbc_scr, 0):
        piece()
    cw = cw_ref[0, g]
    pre3 = pxbc_scr[0, pl.ds(HALO, tm), :].reshape(n_seq, lt, XBC_G)
    xp = jnp.concatenate([halo_ref[0], pre3], axis=1)
    acc = cb_ref[0, g] + xp[:, HALO - 3:HALO - 3 + lt] * cw[0:1]
    for k in (1, 2, 3):
        acc = acc + xp[:, HALO - 3 + k:HALO - 3 + k + lt] * cw[k:k + 1]
    cst_ref[0] = pre3[:, lt - HALO:]
    xbc = _silu(acc.reshape(tm, XBC_G))
    xs = xbc[:, :GROUP_W]
    xsb = xs.astype(BF16)
    bb = xbc[:, GROUP_W:GROUP_W + D_STATE].astype(BF16)
    cc = xbc[:, GROUP_W + D_STATE:].astype(BF16)
    ii = lax.broadcasted_iota(jnp.int32, (tm, tm), 0)
    jj = lax.broadcasted_iota(jnp.int32, (tm, tm), 1)
    mask = ((ii >> lt_shift) == (jj >> lt_shift)) & (ii >= jj)
    acs = _group_lanes(acs_scr[...], g)
    scale = _head_cols(_group_lanes(eacs_scr[...], g), GROUP_W)
    dend = _head_cols(_group_lanes(dend_scr[...], g), GROUP_W)
    cd = _group_lanes(etot_scr[...], g)
    cbm = _dot_nt(cc, bb)
    lane_head = lax.broadcasted_iota(jnp.int32, (tm, GROUP_W), 1) >> 6
    ydiag = None
    for e in range(HEADS_PER_GROUP):
        mp = _intra(cbm, acs, e, _head_row(acs_t_scr, g, e, 0, tm), _head_row(dt_t_scr, g, e, 0, tm), mask)
        d = _dot(mp, jnp.where(lane_head == e, xsb, jnp.zeros((tm, GROUP_W), BF16)))
        ydiag = d if ydiag is None else ydiag + d
    xdtd = (xs * dend).astype(BF16)
    row_seq = lax.broadcasted_iota(jnp.int32, (tm, D_STATE), 0) >> lt_shift
    zero = jnp.zeros((tm, D_STATE), BF16)
    h0 = [h0_ref[0, i].reshape(GROUP_W, D_STATE) for i in range(n_seq)]
    h0_wide = jnp.concatenate(h0, axis=1).astype(BF16)
    c_wide = jnp.concatenate([jnp.where(row_seq == i, cc, zero) for i in range(n_seq)], axis=1)
    b_wide = jnp.concatenate([jnp.where(row_seq == i, bb, zero) for i in range(n_seq)], axis=1)
    yoff = _dot_nt(c_wide, h0_wide) * scale
    s_new = _dot_tn(xdtd, b_wide)
    for i in range(n_seq):
        hn = h0[i] * _head_rows(cd[i * lt:i * lt + 1, :]) + s_new[:, i * D_STATE:(i + 1) * D_STATE]
        hfin_ref[i] = hn.reshape(HEADS_PER_GROUP, SSD_HEAD_DIM, D_STATE)
    _ssd_finish(g, ydiag + yoff, xs, pz_scr[0], dvec_ref, nw_ref, wo_ref, xo_ref)


def _ssd_weight_specs(l, g1, g2):
    whole = lambda rows, width: pl.BlockSpec((1, SSD_GROUPS, rows, width), lambda *a: (l, 0, 0, 0))
    col = lambda width, start: pl.BlockSpec((1, D_MODEL, width), lambda *a: (l, 0, start // width + g1(*a)))
    return [
        col(GROUP_W, 0), col(GROUP_W, S_X), col(D_STATE, S_B), col(D_STATE, S_C),
        pl.BlockSpec((1, D_MODEL, LANES), lambda *a: (l, 0, S_DT // LANES)),
        pl.BlockSpec((1, 1, LANES), lambda *a: (l, 0, 0)),
        pl.BlockSpec((1, 1, LANES), lambda *a: (l, 0, 0)),
        whole(SSD_CONV, XBC_G), whole(1, XBC_G), whole(1, GROUP_W), whole(1, GROUP_W),
        pl.BlockSpec((1, 1, GROUP_W, D_MODEL), lambda *a: (l, g2(*a), 0, 0)),
    ]


def _ssd_weight_args(w):
    return [w["w_in"]] * 5 + [w["dtb"], w["alog"], w["ssd_cw"], w["ssd_cb"], w["dvec"], w["ssd_nw"], w["wo"]]


def _ssd_scratch(tm, slots):
    return ([pltpu.VMEM((slots, tm, GROUP_W), F32), pltpu.VMEM((slots, HALO + tm, XBC_G), F32)]
            + [pltpu.VMEM((tm, LANES), F32)] * 4 + [pltpu.VMEM((LANES, tm), F32)] * 2)


def _ssd_sample_call(l, h, xg, w, halo, h0, *, n_seq, lt):
    m = h.shape[0]
    tm = n_seq * lt
    nt = m // tm
    n_all = m // lt
    kern = functools.partial(_ssd_sample_kernel, n_seq=n_seq)
    row = lambda i, g: (i, 0)
    gi = lambda i, g: g
    return pl.pallas_call(
        kern,
        out_shape=(jax.ShapeDtypeStruct((m, D_MODEL), F32),
                   jax.ShapeDtypeStruct((SSD_GROUPS, n_all, HALO, XBC_G), F32),
                   jax.ShapeDtypeStruct((n_all, SSD_HEADS, SSD_HEAD_DIM, D_STATE), F32)),
        grid=(nt, SSD_GROUPS),
        in_specs=[pl.BlockSpec((tm, D_MODEL), row), pl.BlockSpec((tm, D_MODEL), row)]
        + _ssd_weight_specs(l, gi, gi)
        + [pl.BlockSpec((1, n_seq, HALO, XBC_G), lambda i, g: (g, i, 0, 0)),
           pl.BlockSpec((1, n_seq, HEADS_PER_GROUP, SSD_HEAD_DIM, D_STATE), lambda i, g: (l, i, g, 0, 0))],
        out_specs=(
            pl.BlockSpec((tm, D_MODEL), row),
            pl.BlockSpec((1, n_seq, HALO, XBC_G), lambda i, g: (g, i, 0, 0)),
            pl.BlockSpec((n_seq, HEADS_PER_GROUP, SSD_HEAD_DIM, D_STATE), lambda i, g: (i, g, 0, 0)),
        ),
        scratch_shapes=_ssd_scratch(tm, 1),
        compiler_params=pltpu.CompilerParams(
            dimension_semantics=("arbitrary", "arbitrary"), vmem_limit_bytes=VMEM_LIMIT),
        name="ssd_sample_group",
    )(h, xg, *_ssd_weight_args(w), halo, h0)


def _ffn_kernel(*refs, n_seq, carried, final):
    if carried:
        (x_ref, n2_ref, wg_ref, wv_ref, cw_ref, cb_ref, wd_ref, fn_ref,
         xo_ref, lg_ref, lv_ref, h_scr, carry_g, carry_v) = refs
    else:
        (x_ref, n2_ref, wg_ref, wv_ref, cw_ref, cb_ref, wd_ref, fn_ref, hg_ref, hv_ref,
         xo_ref, lg_ref, lv_ref, h_scr) = refs
    nax = 3 if carried else 2
    f = pl.program_id(nax - 1)
    tm = x_ref.shape[0]
    lt = tm // n_seq
    tf = TF

    @pl.when(f == 0)
    def _():
        x = x_ref[...]
        h_scr[...] = _rms(x, n2_ref[0]).astype(BF16)
        xo_ref[...] = x

    def conv(u, halo, cw, cb):
        if n_seq == 1:
            xp = jnp.concatenate([halo[0], u], axis=0)
            acc = cb + xp[HALO - 2:HALO - 2 + tm] * cw[0:1]
            acc = acc + xp[HALO - 1:HALO - 1 + tm] * cw[1:2]
            acc = acc + xp[HALO:] * cw[2:3]
            return acc, u[tm - HALO:].reshape(1, HALO, tf)
        u3 = u.reshape(n_seq, lt, tf)
        xp = jnp.concatenate([halo, u3], axis=1)
        acc = cb + xp[:, HALO - 2:HALO - 2 + lt] * cw[0:1]
        acc = acc + xp[:, HALO - 1:HALO - 1 + lt] * cw[1:2]
        acc = acc + xp[:, HALO:] * cw[2:3]
        return acc.reshape(tm, tf), u3[:, lt - HALO:]

    if carried:
        @pl.when((f == 0) & (pl.program_id(1) == 0))
        def _():
            carry_g[...] = jnp.zeros(carry_g.shape, F32)
            carry_v[...] = jnp.zeros(carry_v.shape, F32)
        halo_g = carry_g[f].reshape(1, HALO, tf)
        halo_v = carry_v[f].reshape(1, HALO, tf)
    else:
        halo_g = hg_ref[...]
        halo_v = hv_ref[...]
    hb = h_scr[...]
    cw = cw_ref[0, f]
    cb = cb_ref[0, f]
    cg, last_g = conv(_dot(hb, wg_ref[0]), halo_g, cw[:, :tf], cb[:, :tf])
    cv, last_v = conv(_dot(hb, wv_ref[0]), halo_v, cw[:, tf:], cb[:, tf:])
    lg_ref[...] = last_g
    lv_ref[...] = last_v
    if carried:
        carry_g[f] = last_g[0]
        carry_v[f] = last_v[0]
    act = (_silu(cg) * cv).astype(BF16)
    xo_ref[...] += _dot(act, wd_ref[0])

    if final:
        @pl.when(f == NF - 1)
        def _():
            xo_ref[...] = _rms(xo_ref[...], fn_ref[...])


def _ffn_call(l, x, w, fnw, halos, *, batch, tm, n_seq, final):
    m = x.shape[0]
    carried = halos is None
    kern = functools.partial(_ffn_kernel, n_seq=n_seq, carried=carried, final=final)
    if carried:
        nt = m // (batch * tm)
        grid = (batch, nt, NF)
        row = lambda b, t, f: (b * nt + t, 0)
        fi = lambda b, t, f: f
        si = lambda b, t, f: b * nt + t
        n_state = batch * nt
    else:
        grid = (m // tm, NF)
        row = lambda i, f: (i, 0)
        fi = lambda i, f: f
        si = lambda i, f: i
        n_state = m // (tm // n_seq)
    in_specs = [
        _x_tile_spec(tm, row, tm > TM),
        pl.BlockSpec((1, 1, D_MODEL), lambda *a: (l, 0, 0)),
        pl.BlockSpec((1, D_MODEL, TF), lambda *a: (l, 0, fi(*a))),
        pl.BlockSpec((1, D_MODEL, TF), lambda *a: (l, 0, NF + fi(*a))),
        pl.BlockSpec((1, NF, FFN_CONV, 2 * TF), lambda *a: (l, 0, 0, 0)),
        pl.BlockSpec((1, NF, 1, 2 * TF), lambda *a: (l, 0, 0, 0)),
        pl.BlockSpec((1, TF, D_MODEL), lambda *a: (l, fi(*a), 0)),
        pl.BlockSpec((1, D_MODEL), lambda *a: (0, 0)),
    ]
    args = [x, w["n2"], w["w_up"], w["w_up"], w["ffn_cw"], w["ffn_cb"], w["w_down"], fnw]
    scratch = [pltpu.VMEM((tm, D_MODEL), BF16)]
    if carried:
        scratch += [pltpu.VMEM((NF, HALO, TF), F32), pltpu.VMEM((NF, HALO, TF), F32)]
    else:
        in_specs += [pl.BlockSpec((n_seq, HALO, TF), lambda *a: (si(*a), 0, fi(*a))),
                     pl.BlockSpec((n_seq, HALO, TF), lambda *a: (si(*a), 0, fi(*a) + NF))]
        args += [halos, halos]
    st_rows = 1 if carried else n_seq
    return pl.pallas_call(
        kern,
        out_shape=(jax.ShapeDtypeStruct((m, D_MODEL), F32),
                   jax.ShapeDtypeStruct((n_state, HALO, D_FF), F32),
                   jax.ShapeDtypeStruct((n_state, HALO, D_FF), F32)),
        grid=grid,
        in_specs=in_specs,
        out_specs=(
            pl.BlockSpec((tm, D_MODEL), row),
            pl.BlockSpec((st_rows, HALO, TF), lambda *a: (si(*a), 0, fi(*a))),
            pl.BlockSpec((st_rows, HALO, TF), lambda *a: (si(*a), 0, fi(*a))),
        ),
        scratch_shapes=scratch,
        compiler_params=pltpu.CompilerParams(
            dimension_semantics=("arbitrary",) * len(grid), vmem_limit_bytes=VMEM_LIMIT),
        name="ffn_prompt" if carried else "ffn_sample",
    )(*args)


def _group_cols(w, n_groups):
    return w.reshape(w.shape[:-1] + (n_groups, w.shape[-1] // n_groups))


def _xbc_to_groups(a):
    xs = _group_cols(a[..., :D_SSD], SSD_GROUPS)
    bs = _group_cols(a[..., D_SSD:D_SSD + SSD_GROUPS * D_STATE], SSD_GROUPS)
    cs = _group_cols(a[..., D_SSD + SSD_GROUPS * D_STATE:], SSD_GROUPS)
    return jnp.concatenate([xs, bs, cs], axis=-1)


def _groups_to_xbc(a):
    lead = a.shape[:-2]
    xs = a[..., :GROUP_W].reshape(lead + (D_SSD,))
    bs = a[..., GROUP_W:GROUP_W + D_STATE].reshape(lead + (SSD_GROUPS * D_STATE,))
    cs = a[..., GROUP_W + D_STATE:].reshape(lead + (SSD_GROUPS * D_STATE,))
    return jnp.concatenate([xs, bs, cs], axis=-1)


def _ffn_blocks(a, nf):
    return jnp.concatenate([_group_cols(a[..., :D_FF], nf), _group_cols(a[..., D_FF:], nf)], axis=-1)


def _pad_halo(state):
    return jnp.pad(state, ((0, 0), (HALO - state.shape[1], 0), (0, 0)))


def _weights(norm1_w, w_in, ssd_conv_w, ssd_conv_b, dt_bias, a_log, ssd_d, ssd_norm_w, gmlp_norm_w, w_out, norm2_w,
             w_up, ffn_conv_w, ffn_conv_b, w_down):
    d = w_in.shape[0]
    return dict(
        n1=norm1_w[:, None, :], n2=norm2_w[:, None, :],
        w_in=w_in.astype(BF16),
        w_uv=w_in[:, :, S_U:].astype(BF16),
        wo=w_out.astype(BF16).reshape(d, 2 * SSD_GROUPS, GROUP_W, D_MODEL),
        w_up=w_up.astype(BF16),
        w_down=w_down.astype(BF16),
        dtb=jnp.pad(dt_bias, ((0, 0), (0, LANES - SSD_HEADS)))[:, None, :],
        alog=jnp.pad(a_log, ((0, 0), (0, LANES - SSD_HEADS)))[:, None, :],
        ssd_cw=_xbc_to_groups(ssd_conv_w).transpose(0, 2, 1, 3),
        ssd_cb=_xbc_to_groups(ssd_conv_b)[:, :, None, :],
        dvec=jnp.repeat(ssd_d, SSD_HEAD_DIM, axis=-1).reshape(d, SSD_GROUPS, 1, GROUP_W),
        ssd_nw=ssd_norm_w.reshape(d, SSD_GROUPS, 1, GROUP_W),
        gnw=gmlp_norm_w.reshape(d, GMLP_GROUPS, 1, GROUP_W),
        ffn_cw=_ffn_blocks(ffn_conv_w, NF).transpose(0, 2, 1, 3),
        ffn_cb=_ffn_blocks(ffn_conv_b, NF)[:, :, None, :],
    )


def kernel(x_prompt, x_sample, state_ssd_conv, state_ssd, state_ffn_conv, norm1_w, w_in, ssd_conv_w, ssd_conv_b,
           dt_bias, a_log, ssd_d, ssd_norm_w, gmlp_norm_w, gmlp_w_s, gmlp_b_s, w_out, norm2_w, w_up, ffn_conv_w,
           ffn_conv_b, w_down, final_norm_w):
    bp, lp, _ = x_prompt.shape
    bs, ls, _ = x_sample.shape
    xp = x_prompt.reshape(bp * lp, D_MODEL)
    xs = x_sample.reshape(bs * ls, D_MODEL)
    ms = bs * ls
    fnw = final_norm_w[None, :]
    ssd_seqs = 16
    ntp = lp // TM
    w = _weights(norm1_w, w_in, ssd_conv_w, ssd_conv_b, dt_bias, a_log, ssd_d, ssd_norm_w, gmlp_norm_w, w_out,
                 norm2_w, w_up, ffn_conv_w, ffn_conv_b, w_down)
    ws_p = gmlp_w_s
    bs_p = gmlp_b_s[:, :, :, None]
    ws_s = jnp.tile(gmlp_w_s[:, :, :ls, :ls], (1, 1, 1, bs))
    bs_s = jnp.tile(gmlp_b_s[:, :, :ls], (1, 1, bs))[:, :, :, None]
    p_conv, p_ssd, p_ffn = [], [], []
    s_conv, s_ssd, s_ffn, s_v = [], [], [], []
    for l in range(DEPTH):
        final = l == DEPTH - 1
        xg, h = _gmlp_tile_call(l, xp, w, ws_p, bs_p, tm=TM)
        x1, cst, hfin = _ssd_prompt_call(l, h, xg, w, batch=bp, tm=TM)
        xp, lg, lv = _ffn_call(l, x1, w, fnw, None, batch=bp, tm=TM_WIDE, n_seq=1, final=final)
        cst, hfin = (a.reshape((bp, ntp) + a.shape[1:])[:, -1] for a in (cst, hfin))
        lg, lv = (a.reshape((bp, lp // TM_WIDE) + a.shape[1:])[:, -1] for a in (lg, lv))
        p_conv.append(_groups_to_xbc(cst[:, :, HALO - (SSD_CONV - 1):, :].transpose(0, 2, 1, 3)))
        p_ssd.append(hfin)
        p_ffn.append(jnp.concatenate([lg, lv], axis=-1)[:, HALO - (FFN_CONV - 1):, :])
        xg, h, vn = _gmlp_call(l, xs, w, ws_s, bs_s, tm=ms, chunk=ms, seg_len=ls, emit_vn=True)
        halo = _pad_halo(_xbc_to_groups(state_ssd_conv[l]).transpose(2, 0, 1, 3).reshape(
            SSD_GROUPS * bs, SSD_CONV - 1, XBC_G)).reshape(SSD_GROUPS, bs, HALO, XBC_G)
        x1, cst, hfin = _ssd_sample_call(l, h, xg, w, halo, state_ssd, n_seq=ssd_seqs, lt=ls)
        xs, lg, lv = _ffn_call(l, x1, w, fnw, _pad_halo(state_ffn_conv[l]), batch=1, tm=ms, n_seq=bs, final=final)
        s_conv.append(_groups_to_xbc(cst[:, :, HALO - (SSD_CONV - 1):, :].transpose(1, 2, 0, 3)))
        s_ssd.append(hfin)
        s_ffn.append(jnp.concatenate([lg, lv], axis=-1)[:, HALO - (FFN_CONV - 1):, :])
        s_v.append(vn.reshape(bs, ls, D_GMLP))
    return (xp.reshape(bp, lp, D_MODEL), xs.reshape(bs, ls, D_MODEL),
            jnp.stack(p_conv), jnp.stack(p_ssd), jnp.stack(p_ffn),
            jnp.stack(s_conv), jnp.stack(s_ssd), jnp.stack(s_ffn), jnp.stack(s_v))
```

```python
import functools

import jax
import jax.numpy as jnp
from jax import lax
from jax.experimental import pallas as pl
from jax.experimental.pallas import tpu as pltpu

D_MODEL = 2048
DEPTH = 2
CHUNK = 64
D_SSD = 2048
SSD_HEAD_DIM = 64
SSD_HEADS = 32
SSD_GROUPS = 8
HEADS_PER_GROUP = 4
D_STATE = 128
SSD_CONV = 4
D_XBC = 4096
D_GMLP = 2048
GMLP_GROUPS = 8
GMLP_CHUNK = 128
D_FF = 5632
FFN_CONV = 3
EPS = 1e-6

LANES = 128
GROUP_W = 256
XBC_G = GROUP_W + 2 * D_STATE
SSD_COLS = GROUP_W + XBC_G
HALO = 8
TM = 512
TM_WIDE = 1024
SSD_Q = 256
PAIR = 2
TF = 512
NF = D_FF // TF
VMEM_LIMIT = 60 * 1024 * 1024

S_X = D_SSD
S_B = S_X + D_SSD
S_C = S_B + SSD_GROUPS * D_STATE
S_DT = S_C + SSD_GROUPS * D_STATE
S_U = S_DT + SSD_HEADS
S_V = S_U + D_GMLP

F32 = jnp.float32
BF16 = jnp.bfloat16


def _dot(a, b):
    return jnp.dot(a, b, preferred_element_type=F32)


def _dot_nt(a, b):
    return lax.dot_general(a, b, (((1,), (1,)), ((), ())), preferred_element_type=F32)


def _dot_tn(a, b):
    return lax.dot_general(a, b, (((0,), (0,)), ((), ())), preferred_element_type=F32)


def _rms(x, w):
    return x * lax.rsqrt(jnp.mean(x * x, axis=-1, keepdims=True) + EPS) * w


def _silu(x):
    return x * jax.nn.sigmoid(x)


def _softplus(x):
    return jnp.maximum(x, 0.0) + jnp.log1p(jnp.exp(-jnp.abs(x)))


def _seg_cumsum(x, seg_len):
    pos = lax.broadcasted_iota(jnp.int32, x.shape, 0) & (seg_len - 1)
    k = 1
    while k < seg_len:
        x = x + jnp.where(pos >= k, pltpu.roll(x, k, 0), 0.0)
        k *= 2
    return x


def _seg_total(x, seg_len):
    n = x.shape[0]
    pos = lax.broadcasted_iota(jnp.int32, x.shape, 0) & (seg_len - 1)
    k = 1
    while k < seg_len:
        x = x + jnp.where((pos & k) == 0, pltpu.roll(x, n - k, 0), pltpu.roll(x, k, 0))
        k *= 2
    return x


def _head_cols(v, width):
    n = v.shape[0]
    lane_head = lax.broadcasted_iota(jnp.int32, (n, width), 1) >> 6
    out = jnp.broadcast_to(v[:, 3:4], (n, width))
    for e in (2, 1, 0):
        out = jnp.where(lane_head == e, jnp.broadcast_to(v[:, e:e + 1], (n, width)), out)
    return out


def _head_rows(row):
    row_head = lax.broadcasted_iota(jnp.int32, (GROUP_W, D_STATE), 0) >> 6
    out = jnp.broadcast_to(row[:, 3:4], (GROUP_W, D_STATE))
    for e in (2, 1, 0):
        out = jnp.where(row_head == e, jnp.broadcast_to(row[:, e:e + 1], (GROUP_W, D_STATE)), out)
    return out


def _gmlp_group(g, h_scr, wu_ref, wv_ref, gnw_ref, ws_ref, bs_ref, wo_ref, xo_ref, vn_ref, *, chunk, seg_len):
    tm = h_scr.shape[0]
    seg_shift = seg_len.bit_length() - 1
    chunk_shift = CHUNK.bit_length() - 1
    val = {}

    def proj_u():
        val["u"] = jax.nn.gelu(_dot(h_scr[...], wu_ref[0]))

    def proj_v():
        vn = _rms(jax.nn.gelu(_dot(h_scr[...], wv_ref[0])), gnw_ref[0, g])
        if vn_ref is not None:
            vn_ref[...] = vn
        val["vnb"] = vn.astype(BF16)

    def gate():
        ii = lax.broadcasted_iota(jnp.int32, (chunk, chunk), 0)
        jj = lax.broadcasted_iota(jnp.int32, (chunk, chunk), 1)
        same_seq = (ii >> seg_shift) == (jj >> seg_shift)
        block_causal = ((jj & (seg_len - 1)) >> chunk_shift) <= ((ii & (seg_len - 1)) >> chunk_shift)
        w_rows = ws_ref[0, 0]
        w_full = w_rows if seg_len == chunk else jnp.concatenate([w_rows] * (chunk // seg_len), axis=0)
        wm = jnp.where(same_seq & block_causal, w_full, 0.0).astype(BF16)
        bias = bs_ref[0, g]
        parts = []
        for c in range(tm // chunk):
            sg = _dot(wm, val["vnb"][c * chunk:(c + 1) * chunk]) + bias
            parts.append(val["u"][c * chunk:(c + 1) * chunk] * sg)
        y = parts[0] if len(parts) == 1 else jnp.concatenate(parts, axis=0)
        xo_ref[...] += _dot(y.astype(BF16), wo_ref[0, 0])

    return [proj_u, proj_v, gate]


def _gmlp_kernel(x_ref, n1_ref, wu_ref, wv_ref, gnw_ref, ws_ref, bs_ref, wo_ref, xo_ref, h_ref, *rest,
                 chunk, seg_len, emit_vn):
    vn_ref = rest[0] if emit_vn else None
    h_scr = rest[-1]
    g = pl.program_id(1)

    @pl.when(g == 0)
    def _():
        x = x_ref[...]
        hb = _rms(x, n1_ref[0]).astype(BF16)
        h_scr[...] = hb
        h_ref[...] = hb
        xo_ref[...] = x

    for piece in _gmlp_group(g, h_scr, wu_ref, wv_ref, gnw_ref, ws_ref, bs_ref, wo_ref, xo_ref, vn_ref,
                             chunk=chunk, seg_len=seg_len):
        piece()


def _gmlp_tile_kernel(x_ref, n1_ref, w_ref, gnw_ref, ws_ref, bs_ref, wo_ref, xo_ref, h_ref, h_scr):
    tm = x_ref.shape[0]
    chunk = GMLP_CHUNK
    chunk_shift = CHUNK.bit_length() - 1
    x = x_ref[...]
    hb = _rms(x, n1_ref[0]).astype(BF16)
    h_scr[...] = hb
    h_ref[...] = hb
    xo_ref[...] = x
    ii = lax.broadcasted_iota(jnp.int32, (chunk, chunk), 0)
    jj = lax.broadcasted_iota(jnp.int32, (chunk, chunk), 1)
    block_causal = (jj >> chunk_shift) <= (ii >> chunk_shift)

    def project(g):
        u = jax.nn.gelu(_dot(h_scr[...], w_ref[0, :, g * GROUP_W:(g + 1) * GROUP_W]))
        v = jax.nn.gelu(_dot(h_scr[...], w_ref[0, :, D_GMLP + g * GROUP_W:D_GMLP + (g + 1) * GROUP_W]))
        return u, v

    def gate(g, u, v):
        vnb = _rms(v, gnw_ref[0, g]).astype(BF16)
        wm = jnp.where(block_causal, ws_ref[0, g], 0.0).astype(BF16)
        bias = bs_ref[0, g]
        parts = []
        for c in range(tm // chunk):
            sg = _dot(wm, vnb[c * chunk:(c + 1) * chunk]) + bias
            parts.append(u[c * chunk:(c + 1) * chunk] * sg)
        y = jnp.concatenate(parts, axis=0)
        xo_ref[...] += _dot(y.astype(BF16), wo_ref[0, g])

    uv = project(0)
    for g in range(GMLP_GROUPS):
        nxt = project(g + 1) if g + 1 < GMLP_GROUPS else None
        gate(g, *uv)
        uv = nxt


def _gmlp_tile_call(l, x, w, ws, bs, *, tm):
    m = x.shape[0]
    once = dict(pipeline_mode=pl.Buffered(1))
    return pl.pallas_call(
        _gmlp_tile_kernel,
        out_shape=(jax.ShapeDtypeStruct((m, D_MODEL), F32), jax.ShapeDtypeStruct((m, D_MODEL), BF16)),
        grid=(m // tm,),
        in_specs=[
            pl.BlockSpec((tm, D_MODEL), lambda i: (i, 0)),
            pl.BlockSpec((1, 1, D_MODEL), lambda i: (l, 0, 0)),
            pl.BlockSpec((1, D_MODEL, 2 * D_GMLP), lambda i: (l, 0, 0), **once),
            pl.BlockSpec((1, GMLP_GROUPS, 1, GROUP_W), lambda i: (l, 0, 0, 0)),
            pl.BlockSpec((1, GMLP_GROUPS, GMLP_CHUNK, GMLP_CHUNK), lambda i: (l, 0, 0, 0), **once),
            pl.BlockSpec((1, GMLP_GROUPS, GMLP_CHUNK, 1), lambda i: (l, 0, 0, 0), **once),
            pl.BlockSpec((1, GMLP_GROUPS, GROUP_W, D_MODEL), lambda i: (l, 1, 0, 0), **once),
        ],
        out_specs=(pl.BlockSpec((tm, D_MODEL), lambda i: (i, 0)), pl.BlockSpec((tm, D_MODEL), lambda i: (i, 0))),
        scratch_shapes=[pltpu.VMEM((tm, D_MODEL), BF16)],
        compiler_params=pltpu.CompilerParams(dimension_semantics=("arbitrary",), vmem_limit_bytes=VMEM_LIMIT),
        name="gmlp_tile",
    )(x, w["n1"], w["w_uv"], w["gnw"], ws, bs, w["wo"])


def _x_tile_spec(tm, index_map, single):
    if single:
        return pl.BlockSpec((tm, D_MODEL), index_map, pipeline_mode=pl.Buffered(1))
    return pl.BlockSpec((tm, D_MODEL), index_map)


def _gmlp_call(l, x, w, ws, bs, *, tm, chunk, seg_len, emit_vn):
    m = x.shape[0]
    nt = m // tm
    kern = functools.partial(_gmlp_kernel, chunk=chunk, seg_len=seg_len, emit_vn=emit_vn)
    out_shape = [jax.ShapeDtypeStruct((m, D_MODEL), F32), jax.ShapeDtypeStruct((m, D_MODEL), BF16)]
    out_specs = [pl.BlockSpec((tm, D_MODEL), lambda i, g: (i, 0)), pl.BlockSpec((tm, D_MODEL), lambda i, g: (i, 0))]
    if emit_vn:
        out_shape.append(jax.ShapeDtypeStruct((m, D_GMLP), F32))
        out_specs.append(pl.BlockSpec((tm, GROUP_W), lambda i, g: (i, g)))
    return pl.pallas_call(
        kern,
        out_shape=tuple(out_shape),
        grid=(nt, GMLP_GROUPS),
        in_specs=[
            _x_tile_spec(tm, lambda i, g: (i, 0), tm > TM),
            pl.BlockSpec((1, 1, D_MODEL), lambda i, g: (l, 0, 0)),
            pl.BlockSpec((1, D_MODEL, GROUP_W), lambda i, g: (l, 0, g)),
            pl.BlockSpec((1, D_MODEL, GROUP_W), lambda i, g: (l, 0, GMLP_GROUPS + g)),
            pl.BlockSpec((1, GMLP_GROUPS, 1, GROUP_W), lambda i, g: (l, 0, 0, 0)),
            pl.BlockSpec((1, 1, seg_len, chunk), lambda i, g: (l, g, 0, 0)),
            pl.BlockSpec((1, GMLP_GROUPS, chunk, 1), lambda i, g: (l, 0, 0, 0)),
            pl.BlockSpec((1, 1, GROUP_W, D_MODEL), lambda i, g: (l, SSD_GROUPS + g, 0, 0)),
        ],
        out_specs=tuple(out_specs),
        scratch_shapes=[pltpu.VMEM((tm, D_MODEL), BF16)],
        compiler_params=pltpu.CompilerParams(
            dimension_semantics=("arbitrary", "arbitrary"), vmem_limit_bytes=VMEM_LIMIT),
        name="gmlp_group",
    )(x, w["n1"], w["w_uv"], w["w_uv"], w["gnw"], ws, bs, w["wo"])


def _ssd_heads(h_ref, wdt_ref, dtb_ref, alog_ref, seg_len, hd):
    acs_scr, eacs_scr, dend_scr, etot_scr, acs_t_scr, dt_t_scr = hd
    dt = _softplus(_dot(h_ref[...], wdt_ref[0]) + dtb_ref[0])
    dta = dt * (-jnp.exp(alog_ref[0]))
    acs = _seg_cumsum(dta, seg_len)
    tot = _seg_total(dta, seg_len)
    acs_scr[...] = acs
    eacs_scr[...] = jnp.exp(acs)
    dend_scr[...] = jnp.exp(tot - acs) * dt
    etot_scr[...] = jnp.exp(tot)
    acs_t_scr[...] = acs.T
    dt_t_scr[...] = dt.T


def _group_lanes(v, g):
    return pltpu.roll(v, (LANES - HEADS_PER_GROUP * g) & (LANES - 1), 1)


def _ssd_stage1(h_ref, w_refs, pz_scr, pxbc_scr, slot):
    wz_ref, wx_ref, wb_ref, wc_ref = w_refs
    tm = h_ref.shape[0]
    rows = pl.ds(HALO, tm)

    def proj_z():
        pz_scr[slot] = _dot(h_ref[...], wz_ref[0])

    def proj_x():
        pxbc_scr[slot, rows, 0:GROUP_W] = _dot(h_ref[...], wx_ref[0])

    def proj_bc():
        w_bc = jnp.concatenate([wb_ref[0], wc_ref[0]], axis=1)
        pxbc_scr[slot, rows, GROUP_W:XBC_G] = _dot(h_ref[...], w_bc)

    return [proj_z, proj_x, proj_bc]


def _ssd_finish(g, y, xs, z, dvec_ref, nw_ref, wo_ref, xo_ref, j=0):
    y = y + dvec_ref[0, g] * xs
    y = y * _silu(z)
    y = _rms(y, nw_ref[0, g])
    xo_ref[...] += _dot(y.astype(BF16), wo_ref[0, j])


def _intra(cb, acs, e, acs_row, dt_row, mask):
    seg = acs[:, e:e + 1] - acs_row
    if mask is not None:
        seg = jnp.where(mask, seg, -jnp.inf)
    return (cb * jnp.exp(seg) * dt_row).astype(BF16)


def _head_row(t_scr, g, e, c0, n):
    return t_scr[pl.ds(HEADS_PER_GROUP * g + e, 1), c0:c0 + n]


def _ssd_stage2_prompt(slot, g, tm, pz_scr, pxbc_scr, hd, carry, state, cw_ref, cb_ref,
                       dvec_ref, nw_ref, wo_ref, xo_ref, cst_ref, hfin_ref, q, j):
    half = q // 2
    acs_scr, eacs_scr, dend_scr, etot_scr, acs_t_scr, dt_t_scr = hd
    cw = cw_ref[0, g]
    pxbc_scr[slot, 0:HALO, :] = carry[g]
    acc = cb_ref[0, g] + pxbc_scr[slot, pl.ds(HALO - 3, tm), :] * cw[0:1]
    for k in (1, 2, 3):
        acc = acc + pxbc_scr[slot, pl.ds(HALO - 3 + k, tm), :] * cw[k:k + 1]
    last = pxbc_scr[slot, pl.ds(tm, HALO), :]
    carry[g] = last
    cst_ref[0, j] = last
    xbc = _silu(acc)
    xs = xbc[:, :GROUP_W]
    xsb = xs.astype(BF16)
    bb = xbc[:, GROUP_W:GROUP_W + D_STATE].astype(BF16)
    cc = xbc[:, GROUP_W + D_STATE:].astype(BF16)
    ii = lax.broadcasted_iota(jnp.int32, (half, half), 0)
    jj = lax.broadcasted_iota(jnp.int32, (half, half), 1)
    tri = ii >= jj
    lane_head = lax.broadcasted_iota(jnp.int32, (q, GROUP_W), 1) >> 6
    hst = state[g]
    ys = []
    for c in range(tm // q):
        r0 = c * q
        sl = slice(r0, r0 + q)
        acs = _group_lanes(acs_scr[sl], g)
        scale = _head_cols(_group_lanes(eacs_scr[sl], g), GROUP_W)
        dend = _head_cols(_group_lanes(dend_scr[sl], g), GROUP_W)
        etot = _group_lanes(etot_scr[r0:r0 + HALO], g)[0:1]
        cbm = _dot_nt(cc[sl], bb[sl])
        top, bot = None, None
        for e in range(HEADS_PER_GROUP):
            arow = _head_row(acs_t_scr, g, e, r0, q)
            drow = _head_row(dt_t_scr, g, e, r0, q)
            xe = jnp.where(lane_head == e, xsb[sl], jnp.zeros((q, GROUP_W), BF16))
            m00 = _intra(cbm[:half, :half], acs[:half], e, arow[:, :half], drow[:, :half], tri)
            m10 = _intra(cbm[half:, :half], acs[half:], e, arow[:, :half], drow[:, :half], None)
            m11 = _intra(cbm[half:, half:], acs[half:], e, arow[:, half:], drow[:, half:], tri)
            d0 = _dot(m00, xe[:half])
            d1 = _dot(jnp.concatenate([m10, m11], axis=1), xe)
            top = d0 if top is None else top + d0
            bot = d1 if bot is None else bot + d1
        ydiag = jnp.concatenate([top, bot], axis=0)
        yoff = _dot_nt(cc[sl], hst.astype(BF16)) * scale
        ys.append(ydiag + yoff)
        s_new = _dot_tn((xs[sl] * dend).astype(BF16), bb[sl])
        hst = hst * _head_rows(etot) + s_new
    state[g] = hst
    hfin_ref[0, j * HEADS_PER_GROUP:(j + 1) * HEADS_PER_GROUP] = hst.reshape(HEADS_PER_GROUP, SSD_HEAD_DIM, D_STATE)
    y = ys[0] if len(ys) == 1 else jnp.concatenate(ys, axis=0)
    _ssd_finish(g, y, xs, pz_scr[slot], dvec_ref, nw_ref, wo_ref, xo_ref, j)


def _ssd_prompt_kernel(h_ref, xg_ref, wz_ref, wx_ref, wb_ref, wc_ref, wdt_ref, dtb_ref, alog_ref, cw_ref, cb_ref,
                       dvec_ref, nw_ref, wo_ref, xo_ref, cst_ref, hfin_ref, pz_scr, pxbc_scr, *scr, q):
    hd, (carry, state) = scr[:6], scr[6:]
    t = pl.program_id(1)
    s = pl.program_id(2)
    tm = h_ref.shape[0]
    rows = pl.ds(HALO, tm)

    @pl.when((s == 0) & (t == 0))
    def _():
        carry[...] = jnp.zeros(carry.shape, F32)
        state[...] = jnp.zeros(state.shape, F32)

    @pl.when(s == 0)
    def _():
        xo_ref[...] = xg_ref[...]
        _ssd_heads(h_ref, wdt_ref, dtb_ref, alog_ref, q, hd)

    for j in range(PAIR):
        pz_scr[j] = _dot(h_ref[...], wz_ref[0, :, j * GROUP_W:(j + 1) * GROUP_W])
        pxbc_scr[j, rows, 0:GROUP_W] = _dot(h_ref[...], wx_ref[0, :, j * GROUP_W:(j + 1) * GROUP_W])
        w_bc = jnp.concatenate([wb_ref[0, :, j * D_STATE:(j + 1) * D_STATE],
                                wc_ref[0, :, j * D_STATE:(j + 1) * D_STATE]], axis=1)
        pxbc_scr[j, rows, GROUP_W:XBC_G] = _dot(h_ref[...], w_bc)
    for j in range(PAIR):
        _ssd_stage2_prompt(j, PAIR * s + j, tm, pz_scr, pxbc_scr, hd, carry, state, cw_ref, cb_ref,
                           dvec_ref, nw_ref, wo_ref, xo_ref, cst_ref, hfin_ref, q, j)


def _ssd_prompt_call(l, h, xg, w, *, batch, tm):
    m = h.shape[0]
    nt = m // (batch * tm)
    row = lambda b, t, s: (b * nt + t, 0)
    col = lambda width, start: pl.BlockSpec((1, D_MODEL, PAIR * width),
                                            lambda b, t, s: (l, 0, start // (PAIR * width) + s))
    whole = lambda r, width: pl.BlockSpec((1, SSD_GROUPS, r, width), lambda b, t, s: (l, 0, 0, 0))
    in_specs = [
        pl.BlockSpec((tm, D_MODEL), row), pl.BlockSpec((tm, D_MODEL), row),
        col(GROUP_W, 0), col(GROUP_W, S_X), col(D_STATE, S_B), col(D_STATE, S_C),
        pl.BlockSpec((1, D_MODEL, LANES), lambda b, t, s: (l, 0, S_DT // LANES)),
        pl.BlockSpec((1, 1, LANES), lambda b, t, s: (l, 0, 0)),
        pl.BlockSpec((1, 1, LANES), lambda b, t, s: (l, 0, 0)),
        whole(SSD_CONV, XBC_G), whole(1, XBC_G), whole(1, GROUP_W), whole(1, GROUP_W),
        pl.BlockSpec((1, PAIR, GROUP_W, D_MODEL), lambda b, t, s: (l, s, 0, 0)),
    ]
    return pl.pallas_call(
        functools.partial(_ssd_prompt_kernel, q=SSD_Q),
        out_shape=(jax.ShapeDtypeStruct((m, D_MODEL), F32),
                   jax.ShapeDtypeStruct((batch * nt, SSD_GROUPS, HALO, XBC_G), F32),
                   jax.ShapeDtypeStruct((batch * nt, SSD_HEADS, SSD_HEAD_DIM, D_STATE), F32)),
        grid=(batch, nt, SSD_GROUPS // PAIR),
        in_specs=in_specs,
        out_specs=(
            pl.BlockSpec((tm, D_MODEL), row),
            pl.BlockSpec((1, PAIR, HALO, XBC_G), lambda b, t, s: (b * nt + t, s, 0, 0)),
            pl.BlockSpec((1, PAIR * HEADS_PER_GROUP, SSD_HEAD_DIM, D_STATE), lambda b, t, s: (b * nt + t, s, 0, 0)),
        ),
        scratch_shapes=_ssd_scratch(tm, PAIR)
        + [pltpu.VMEM((SSD_GROUPS, HALO, XBC_G), F32), pltpu.VMEM((SSD_GROUPS, GROUP_W, D_STATE), F32)],
        compiler_params=pltpu.CompilerParams(
            dimension_semantics=("arbitrary", "arbitrary", "arbitrary"), vmem_limit_bytes=VMEM_LIMIT),
        name="ssd_prompt_group",
    )(h, xg, *_ssd_weight_args(w))


def _ssd_sample_kernel(h_ref, xg_ref, wz_ref, wx_ref, wb_ref, wc_ref, wdt_ref, dtb_ref, alog_ref, cw_ref, cb_ref, dvec_ref, nw_ref, wo_ref,
                       halo_ref, h0_ref, xo_ref, cst_ref, hfin_ref, pz_scr, pxbc_scr, *hd, n_seq):
    acs_scr, eacs_scr, dend_scr, etot_scr, acs_t_scr, dt_t_scr = hd
    g = pl.program_id(1)
    tm = h_ref.shape[0]
    lt = tm // n_seq
    lt_shift = lt.bit_length() - 1

    @pl.when(g == 0)
    def _():
        xo_ref[...] = xg_ref[...]
        _ssd_heads(h_ref, wdt_ref, dtb_ref, alog_ref, lt, hd)

    for piece in _ssd_stage1(h_ref, (wz_ref, wx_ref, wb_ref, wc_ref), pz_scr, pxbc_scr, 0):
        piece()
    cw = cw_ref[0, g]
    pre3 = pxbc_scr[0, pl.ds(HALO, tm), :].reshape(n_seq, lt, XBC_G)
    xp = jnp.concatenate([halo_ref[0], pre3], axis=1)
    acc = cb_ref[0, g] + xp[:, HALO - 3:HALO - 3 + lt] * cw[0:1]
    for k in (1, 2, 3):
        acc = acc + xp[:, HALO - 3 + k:HALO - 3 + k + lt] * cw[k:k + 1]
    cst_ref[0] = pre3[:, lt - HALO:]
    xbc = _silu(acc.reshape(tm, XBC_G))
    xs = xbc[:, :GROUP_W]
    xsb = xs.astype(BF16)
    bb = xbc[:, GROUP_W:GROUP_W + D_STATE].astype(BF16)
    cc = xbc[:, GROUP_W + D_STATE:].astype(BF16)
    ii = lax.broadcasted_iota(jnp.int32, (tm, tm), 0)
    jj = lax.broadcasted_iota(jnp.int32, (tm, tm), 1)
    mask = ((ii >> lt_shift) == (jj >> lt_shift)) & (ii >= jj)
    acs = _group_lanes(acs_scr[...], g)
    scale = _head_cols(_group_lanes(eacs_scr[...], g), GROUP_W)
    dend = _head_cols(_group_lanes(dend_scr[...], g), GROUP_W)
    cd = _group_lanes(etot_scr[...], g)
    cbm = _dot_nt(cc, bb)
    lane_head = lax.broadcasted_iota(jnp.int32, (tm, GROUP_W), 1) >> 6
    ydiag = None
    for e in range(HEADS_PER_GROUP):
        mp = _intra(cbm, acs, e, _head_row(acs_t_scr, g, e, 0, tm), _head_row(dt_t_scr, g, e, 0, tm), mask)
        d = _dot(mp, jnp.where(lane_head == e, xsb, jnp.zeros((tm, GROUP_W), BF16)))
        ydiag = d if ydiag is None else ydiag + d
    xdtd = (xs * dend).astype(BF16)
    row_seq = lax.broadcasted_iota(jnp.int32, (tm, D_STATE), 0) >> lt_shift
    zero = jnp.zeros((tm, D_STATE), BF16)
    h0 = [h0_ref[0, i].reshape(GROUP_W, D_STATE) for i in range(n_seq)]
    h0_wide = jnp.concatenate(h0, axis=1).astype(BF16)
    c_wide = jnp.concatenate([jnp.where(row_seq == i, cc, zero) for i in range(n_seq)], axis=1)
    b_wide = jnp.concatenate([jnp.where(row_seq == i, bb, zero) for i in range(n_seq)], axis=1)
    yoff = _dot_nt(c_wide, h0_wide) * scale
    s_new = _dot_tn(xdtd, b_wide)
    for i in range(n_seq):
        hn = h0[i] * _head_rows(cd[i * lt:i * lt + 1, :]) + s_new[:, i * D_STATE:(i + 1) * D_STATE]
        hfin_ref[i] = hn.reshape(HEADS_PER_GROUP, SSD_HEAD_DIM, D_STATE)
    _ssd_finish(g, ydiag + yoff, xs, pz_scr[0], dvec_ref, nw_ref, wo_ref, xo_ref)


def _ssd_weight_specs(l, g1, g2):
    whole = lambda rows, width: pl.BlockSpec((1, SSD_GROUPS, rows, width), lambda *a: (l, 0, 0, 0))
    col = lambda width, start: pl.BlockSpec((1, D_MODEL, width), lambda *a: (l, 0, start // width + g1(*a)))
    return [
        col(GROUP_W, 0), col(GROUP_W, S_X), col(D_STATE, S_B), col(D_STATE, S_C),
        pl.BlockSpec((1, D_MODEL, LANES), lambda *a: (l, 0, S_DT // LANES)),
        pl.BlockSpec((1, 1, LANES), lambda *a: (l, 0, 0)),
        pl.BlockSpec((1, 1, LANES), lambda *a: (l, 0, 0)),
        whole(SSD_CONV, XBC_G), whole(1, XBC_G), whole(1, GROUP_W), whole(1, GROUP_W),
        pl.BlockSpec((1, 1, GROUP_W, D_MODEL), lambda *a: (l, g2(*a), 0, 0)),
    ]


def _ssd_weight_args(w):
    return [w["w_in"]] * 5 + [w["dtb"], w["alog"], w["ssd_cw"], w["ssd_cb"], w["dvec"], w["ssd_nw"], w["wo"]]


def _ssd_scratch(tm, slots):
    return ([pltpu.VMEM((slots, tm, GROUP_W), F32), pltpu.VMEM((slots, HALO + tm, XBC_G), F32)]
            + [pltpu.VMEM((tm, LANES), F32)] * 4 + [pltpu.VMEM((LANES, tm), F32)] * 2)


def _ssd_sample_call(l, h, xg, w, halo, h0, *, n_seq, lt):
    m = h.shape[0]
    tm = n_seq * lt
    nt = m // tm
    n_all = m // lt
    kern = functools.partial(_ssd_sample_kernel, n_seq=n_seq)
    row = lambda i, g: (i, 0)
    gi = lambda i, g: g
    return pl.pallas_call(
        kern,
        out_shape=(jax.ShapeDtypeStruct((m, D_MODEL), F32),
                   jax.ShapeDtypeStruct((SSD_GROUPS, n_all, HALO, XBC_G), F32),
                   jax.ShapeDtypeStruct((n_all, SSD_HEADS, SSD_HEAD_DIM, D_STATE), F32)),
        grid=(nt, SSD_GROUPS),
        in_specs=[pl.BlockSpec((tm, D_MODEL), row), pl.BlockSpec((tm, D_MODEL), row)]
        + _ssd_weight_specs(l, gi, gi)
        + [pl.BlockSpec((1, n_seq, HALO, XBC_G), lambda i, g: (g, i, 0, 0)),
           pl.BlockSpec((1, n_seq, HEADS_PER_GROUP, SSD_HEAD_DIM, D_STATE), lambda i, g: (l, i, g, 0, 0))],
        out_specs=(
            pl.BlockSpec((tm, D_MODEL), row),
            pl.BlockSpec((1, n_seq, HALO, XBC_G), lambda i, g: (g, i, 0, 0)),
            pl.BlockSpec((n_seq, HEADS_PER_GROUP, SSD_HEAD_DIM, D_STATE), lambda i, g: (i, g, 0, 0)),
        ),
        scratch_shapes=_ssd_scratch(tm, 1),
        compiler_params=pltpu.CompilerParams(
            dimension_semantics=("arbitrary", "arbitrary"), vmem_limit_bytes=VMEM_LIMIT),
        name="ssd_sample_group",
    )(h, xg, *_ssd_weight_args(w), halo, h0)


def _ffn_kernel(*refs, n_seq, carried, final):
    if carried:
        (x_ref, n2_ref, wg_ref, wv_ref, cw_ref, cb_ref, wd_ref, fn_ref,
         xo_ref, lg_ref, lv_ref, h_scr, carry_g, carry_v) = refs
    else:
        (x_ref, n2_ref, wg_ref, wv_ref, cw_ref, cb_ref, wd_ref, fn_ref, hg_ref, hv_ref,
         xo_ref, lg_ref, lv_ref, h_scr) = refs
    nax = 3 if carried else 2
    f = pl.program_id(nax - 1)
    tm = x_ref.shape[0]
    lt = tm // n_seq
    tf = TF

    @pl.when(f == 0)
    def _():
        x = x_ref[...]
        h_scr[...] = _rms(x, n2_ref[0]).astype(BF16)
        xo_ref[...] = x

    def conv(u, halo, cw, cb):
        if n_seq == 1:
            xp = jnp.concatenate([halo[0], u], axis=0)
            acc = cb + xp[HALO - 2:HALO - 2 + tm] * cw[0:1]
            acc = acc + xp[HALO - 1:HALO - 1 + tm] * cw[1:2]
            acc = acc + xp[HALO:] * cw[2:3]
            return acc, u[tm - HALO:].reshape(1, HALO, tf)
        u3 = u.reshape(n_seq, lt, tf)
        xp = jnp.concatenate([halo, u3], axis=1)
        acc = cb + xp[:, HALO - 2:HALO - 2 + lt] * cw[0:1]
        acc = acc + xp[:, HALO - 1:HALO - 1 + lt] * cw[1:2]
        acc = acc + xp[:, HALO:] * cw[2:3]
        return acc.reshape(tm, tf), u3[:, lt - HALO:]

    if carried:
        @pl.when((f == 0) & (pl.program_id(1) == 0))
        def _():
            carry_g[...] = jnp.zeros(carry_g.shape, F32)
            carry_v[...] = jnp.zeros(carry_v.shape, F32)
        halo_g = carry_g[f].reshape(1, HALO, tf)
        halo_v = carry_v[f].reshape(1, HALO, tf)
    else:
        halo_g = hg_ref[...]
        halo_v = hv_ref[...]
    hb = h_scr[...]
    cw = cw_ref[0, f]
    cb = cb_ref[0, f]
    cg, last_g = conv(_dot(hb, wg_ref[0]), halo_g, cw[:, :tf], cb[:, :tf])
    cv, last_v = conv(_dot(hb, wv_ref[0]), halo_v, cw[:, tf:], cb[:, tf:])
    lg_ref[...] = last_g
    lv_ref[...] = last_v
    if carried:
        carry_g[f] = last_g[0]
        carry_v[f] = last_v[0]
    act = (_silu(cg) * cv).astype(BF16)
    xo_ref[...] += _dot(act, wd_ref[0])

    if final:
        @pl.when(f == NF - 1)
        def _():
            xo_ref[...] = _rms(xo_ref[...], fn_ref[...])


def _ffn_call(l, x, w, fnw, halos, *, batch, tm, n_seq, final):
    m = x.shape[0]
    carried = halos is None
    kern = functools.partial(_ffn_kernel, n_seq=n_seq, carried=carried, final=final)
    if carried:
        nt = m // (batch * tm)
        grid = (batch, nt, NF)
        row = lambda b, t, f: (b * nt + t, 0)
        fi = lambda b, t, f: f
        si = lambda b, t, f: b * nt + t
        n_state = batch * nt
    else:
        grid = (m // tm, NF)
        row = lambda i, f: (i, 0)
        fi = lambda i, f: f
        si = lambda i, f: i
        n_state = m // (tm // n_seq)
    in_specs = [
        _x_tile_spec(tm, row, False),
        pl.BlockSpec((1, 1, D_MODEL), lambda *a: (l, 0, 0)),
        pl.BlockSpec((1, D_MODEL, TF), lambda *a: (l, 0, fi(*a))),
        pl.BlockSpec((1, D_MODEL, TF), lambda *a: (l, 0, NF + fi(*a))),
        pl.BlockSpec((1, NF, FFN_CONV, 2 * TF), lambda *a: (l, 0, 0, 0)),
        pl.BlockSpec((1, NF, 1, 2 * TF), lambda *a: (l, 0, 0, 0)),
        pl.BlockSpec((1, TF, D_MODEL), lambda *a: (l, fi(*a), 0)),
        pl.BlockSpec((1, D_MODEL), lambda *a: (0, 0)),
    ]
    args = [x, w["n2"], w["w_up"], w["w_up"], w["ffn_cw"], w["ffn_cb"], w["w_down"], fnw]
    scratch = [pltpu.VMEM((tm, D_MODEL), BF16)]
    if carried:
        scratch += [pltpu.VMEM((NF, HALO, TF), F32), pltpu.VMEM((NF, HALO, TF), F32)]
    else:
        in_specs += [pl.BlockSpec((n_seq, HALO, TF), lambda *a: (si(*a), 0, fi(*a))),
                     pl.BlockSpec((n_seq, HALO, TF), lambda *a: (si(*a), 0, fi(*a) + NF))]
        args += [halos, halos]
    st_rows = 1 if carried else n_seq
    return pl.pallas_call(
        kern,
        out_shape=(jax.ShapeDtypeStruct((m, D_MODEL), F32),
                   jax.ShapeDtypeStruct((n_state, HALO, D_FF), F32),
                   jax.ShapeDtypeStruct((n_state, HALO, D_FF), F32)),
        grid=grid,
        in_specs=in_specs,
        out_specs=(
            pl.BlockSpec((tm, D_MODEL), row),
            pl.BlockSpec((st_rows, HALO, TF), lambda *a: (si(*a), 0, fi(*a))),
            pl.BlockSpec((st_rows, HALO, TF), lambda *a: (si(*a), 0, fi(*a))),
        ),
        scratch_shapes=scratch,
        compiler_params=pltpu.CompilerParams(
            dimension_semantics=("arbitrary",) * len(grid), vmem_limit_bytes=VMEM_LIMIT),
        name="ffn_prompt" if carried else "ffn_sample",
    )(*args)


def _group_cols(w, n_groups):
    return w.reshape(w.shape[:-1] + (n_groups, w.shape[-1] // n_groups))


def _xbc_to_groups(a):
    xs = _group_cols(a[..., :D_SSD], SSD_GROUPS)
    bs = _group_cols(a[..., D_SSD:D_SSD + SSD_GROUPS * D_STATE], SSD_GROUPS)
    cs = _group_cols(a[..., D_SSD + SSD_GROUPS * D_STATE:], SSD_GROUPS)
    return jnp.concatenate([xs, bs, cs], axis=-1)


def _groups_to_xbc(a):
    lead = a.shape[:-2]
    xs = a[..., :GROUP_W].reshape(lead + (D_SSD,))
    bs = a[..., GROUP_W:GROUP_W + D_STATE].reshape(lead + (SSD_GROUPS * D_STATE,))
    cs = a[..., GROUP_W + D_STATE:].reshape(lead + (SSD_GROUPS * D_STATE,))
    return jnp.concatenate([xs, bs, cs], axis=-1)


def _ffn_blocks(a, nf):
    return jnp.concatenate([_group_cols(a[..., :D_FF], nf), _group_cols(a[..., D_FF:], nf)], axis=-1)


def _pad_halo(state):
    return jnp.pad(state, ((0, 0), (HALO - state.shape[1], 0), (0, 0)))


def _weights(norm1_w, w_in, ssd_conv_w, ssd_conv_b, dt_bias, a_log, ssd_d, ssd_norm_w, gmlp_norm_w, w_out, norm2_w,
             w_up, ffn_conv_w, ffn_conv_b, w_down):
    d = w_in.shape[0]
    return dict(
        n1=norm1_w[:, None, :], n2=norm2_w[:, None, :],
        w_in=w_in.astype(BF16),
        w_uv=w_in[:, :, S_U:].astype(BF16),
        wo=w_out.astype(BF16).reshape(d, 2 * SSD_GROUPS, GROUP_W, D_MODEL),
        w_up=w_up.astype(BF16),
        w_down=w_down.astype(BF16),
        dtb=jnp.pad(dt_bias, ((0, 0), (0, LANES - SSD_HEADS)))[:, None, :],
        alog=jnp.pad(a_log, ((0, 0), (0, LANES - SSD_HEADS)))[:, None, :],
        ssd_cw=_xbc_to_groups(ssd_conv_w).transpose(0, 2, 1, 3),
        ssd_cb=_xbc_to_groups(ssd_conv_b)[:, :, None, :],
        dvec=jnp.repeat(ssd_d, SSD_HEAD_DIM, axis=-1).reshape(d, SSD_GROUPS, 1, GROUP_W),
        ssd_nw=ssd_norm_w.reshape(d, SSD_GROUPS, 1, GROUP_W),
        gnw=gmlp_norm_w.reshape(d, GMLP_GROUPS, 1, GROUP_W),
        ffn_cw=_ffn_blocks(ffn_conv_w, NF).transpose(0, 2, 1, 3),
        ffn_cb=_ffn_blocks(ffn_conv_b, NF)[:, :, None, :],
    )


def kernel(x_prompt, x_sample, state_ssd_conv, state_ssd, state_ffn_conv, norm1_w, w_in, ssd_conv_w, ssd_conv_b,
           dt_bias, a_log, ssd_d, ssd_norm_w, gmlp_norm_w, gmlp_w_s, gmlp_b_s, w_out, norm2_w, w_up, ffn_conv_w,
           ffn_conv_b, w_down, final_norm_w):
    bp, lp, _ = x_prompt.shape
    bs, ls, _ = x_sample.shape
    xp = x_prompt.reshape(bp * lp, D_MODEL)
    xs = x_sample.reshape(bs * ls, D_MODEL)
    ms = bs * ls
    fnw = final_norm_w[None, :]
    ssd_seqs = 16
    ntp = lp // TM
    w = _weights(norm1_w, w_in, ssd_conv_w, ssd_conv_b, dt_bias, a_log, ssd_d, ssd_norm_w, gmlp_norm_w, w_out,
                 norm2_w, w_up, ffn_conv_w, ffn_conv_b, w_down)
    ws_p = gmlp_w_s
    bs_p = gmlp_b_s[:, :, :, None]
    ws_s = jnp.tile(gmlp_w_s[:, :, :ls, :ls], (1, 1, 1, bs))
    bs_s = jnp.tile(gmlp_b_s[:, :, :ls], (1, 1, bs))[:, :, :, None]
    p_conv, p_ssd, p_ffn = [], [], []
    s_conv, s_ssd, s_ffn, s_v = [], [], [], []
    for l in range(DEPTH):
        final = l == DEPTH - 1
        xg, h = _gmlp_tile_call(l, xp, w, ws_p, bs_p, tm=TM)
        x1, cst, hfin = _ssd_prompt_call(l, h, xg, w, batch=bp, tm=TM)
        xp, lg, lv = _ffn_call(l, x1, w, fnw, None, batch=bp, tm=TM_WIDE, n_seq=1, final=final)
        cst, hfin = (a.reshape((bp, ntp) + a.shape[1:])[:, -1] for a in (cst, hfin))
        lg, lv = (a.reshape((bp, lp // TM_WIDE) + a.shape[1:])[:, -1] for a in (lg, lv))
        p_conv.append(_groups_to_xbc(cst[:, :, HALO - (SSD_CONV - 1):, :].transpose(0, 2, 1, 3)))
        p_ssd.append(hfin)
        p_ffn.append(jnp.concatenate([lg, lv], axis=-1)[:, HALO - (FFN_CONV - 1):, :])
        xg, h, vn = _gmlp_call(l, xs, w, ws_s, bs_s, tm=ms, chunk=ms, seg_len=ls, emit_vn=True)
        halo = _pad_halo(_xbc_to_groups(state_ssd_conv[l]).transpose(2, 0, 1, 3).reshape(
            SSD_GROUPS * bs, SSD_CONV - 1, XBC_G)).reshape(SSD_GROUPS, bs, HALO, XBC_G)
        x1, cst, hfin = _ssd_sample_call(l, h, xg, w, halo, state_ssd, n_seq=ssd_seqs, lt=ls)
        xs, lg, lv = _ffn_call(l, x1, w, fnw, _pad_halo(state_ffn_conv[l]), batch=1, tm=ms, n_seq=bs, final=final)
        s_conv.append(_groups_to_xbc(cst[:, :, HALO - (SSD_CONV - 1):, :].transpose(1, 2, 0, 3)))
        s_ssd.append(hfin)
        s_ffn.append(jnp.concatenate([lg, lv], axis=-1)[:, HALO - (FFN_CONV - 1):, :])
        s_v.append(vn.reshape(bs, ls, D_GMLP))
    return (xp.reshape(bp, lp, D_MODEL), xs.reshape(bs, ls, D_MODEL),
            jnp.stack(p_conv), jnp.stack(p_ssd), jnp.stack(p_ffn),
            jnp.stack(s_conv), jnp.stack(s_ssd), jnp.stack(s_ffn), jnp.stack(s_v))
```
